```python
import jax, jax.numpy as jnp
from jax import lax
import numpy as np

D_MODEL = 2048
BATCH = 4
SEQ = 4096
DEPTH = 4
DEC_BATCH = 16
DEC_SEQ = 2048
PAST_LEN = 128

MIX_W = D_MODEL
GLA_HEADS = 4
GLA_DV = D_MODEL // 2 // GLA_HEADS
GLA_DK = GLA_DV // 2
GLA_QK = GLA_HEADS * GLA_DK
GLA_W = GLA_HEADS * GLA_DV
GLA_RANK = 16
GLA_TAU = 16.0
GLA_CHUNK = 64
CONV_CH = D_MODEL // 2
CONV_WIDTH = 31
RET_HEADS = 4
RET_DK = D_MODEL // 2 // RET_HEADS
RET_DV = D_MODEL // 2 // RET_HEADS
RET_QK = RET_HEADS * RET_DK
RET_W = RET_HEADS * RET_DV
RET_CHUNK = 128
SGU_CH = D_MODEL // 2
SGU_GROUPS = 4
SGU_CHUNK = 128
N_EXPERTS = 16
CAP_FACTOR = 2
D_EXPERT = 2816
ALPHA = (2 * DEPTH) ** 0.25
BETA = (8 * DEPTH) ** -0.25
LN_EPS = 1e-5
N_EVEN = (DEPTH + 1) // 2
N_ODD = DEPTH // 2
EV_K = GLA_QK
EV_V = 2 * GLA_QK
EV_G = EV_V + GLA_W
EV_LR = EV_G + GLA_W
EV_CV = EV_LR + 2 * GLA_RANK
EVEN_COLS = EV_CV + 2 * CONV_CH
OD_K = RET_QK
OD_V = 2 * RET_QK
OD_G = OD_V + RET_W
OD_S = OD_G + RET_W
ODD_COLS = OD_S + 2 * SGU_CH

kernel_name = 'hybrid_bidir_gla_conv_retention_sgu_ecmoe'


def _layernorm(x, g, b):
    xf = x.astype(jnp.float32)
    mu = xf.mean(-1, keepdims=True)
    var = jnp.mean(jnp.square(xf - mu), -1, keepdims=True)
    return (xf - mu) * lax.rsqrt(var + LN_EPS) * g.astype(jnp.float32) + b.astype(jnp.float32)


def _headnorm(x, g):
    mu = x.mean(-1, keepdims=True)
    var = jnp.mean(jnp.square(x - mu), -1, keepdims=True)
    return (x - mu) * lax.rsqrt(var + LN_EPS) * g.astype(jnp.float32)


def _to_chunks(t, c):
    b, l, h, d = t.shape
    return t.reshape(b, l // c, c, h, d).transpose(1, 0, 3, 2, 4)


def _from_chunks(t):
    n, b, h, c, d = t.shape
    return t.transpose(1, 0, 3, 2, 4).reshape(b, n * c, h, d)


def _flip(t):
    return jnp.flip(t, axis=1)


def _gla_scan(q, k, v, log_a):
    b_, l_, h_, dk = q.shape
    dv = v.shape[-1]
    c = GLA_CHUNK
    mask = jnp.tril(jnp.ones((c, c), bool))[:, :, None]

    def step(s, inp):
        qi, ki, vi, ai = inp
        cum = jnp.cumsum(ai, axis=2)
        diff = cum[:, :, :, None, :] - cum[:, :, None, :, :]
        decay = jnp.exp(jnp.where(mask, diff, -jnp.inf))
        att = jnp.einsum('bhid,bhjd,bhijd->bhij', qi, ki, decay)
        o = jnp.einsum('bhij,bhjv->bhiv', att, vi) + jnp.einsum('bhid,bhdv->bhiv', qi * jnp.exp(cum), s)
        last = cum[:, :, -1:, :]
        s = jnp.exp(last[:, :, 0, :, None]) * s + jnp.einsum('bhjd,bhjv->bhdv', ki * jnp.exp(last - cum), vi)
        return s, o

    s0 = jnp.zeros((b_, h_, dk, dv), jnp.float32)
    _, o = lax.scan(step, s0, (_to_chunks(q, c), _to_chunks(k, c), _to_chunks(v, c), _to_chunks(log_a, c)))
    return _from_chunks(o)


def _ret_scan(q, k, v, log_g):
    b_, l_, h_, dk = q.shape
    dv = v.shape[-1]
    c = RET_CHUNK
    idx = jnp.arange(c, dtype=jnp.float32)
    diff = idx[:, None] - idx[None, :]
    dmat = jnp.where(diff >= 0, jnp.exp(log_g[:, None, None] * jnp.maximum(diff, 0.0)), 0.0)
    q_dec = jnp.exp(log_g[:, None] * (idx + 1.0))[None, :, :, None]
    k_dec = jnp.exp(log_g[:, None] * (c - 1.0 - idx))[None, :, :, None]
    chunk_dec = jnp.exp(log_g * c)[None, :, None, None]

    def step(s, inp):
        qi, ki, vi = inp
        att = jnp.einsum('bhid,bhjd->bhij', qi, ki) * dmat[None]
        o = jnp.einsum('bhij,bhjv->bhiv', att, vi) + jnp.einsum('bhid,bhdv->bhiv', qi, s) * q_dec
        s = chunk_dec * s + jnp.einsum('bhjd,bhjv->bhdv', ki * k_dec, vi)
        return s, o

    s0 = jnp.zeros((b_, h_, dk, dv), jnp.float32)
    _, o = lax.scan(step, s0, (_to_chunks(q, c), _to_chunks(k, c), _to_chunks(v, c)))
    return _from_chunks(o)


def _rotary(t):
    l_, d = t.shape[1], t.shape[-1]
    inv = 1.0 / (10000.0 ** jnp.linspace(0.0, 1.0, d // 2, dtype=jnp.float32))
    ang = jnp.arange(l_, dtype=jnp.float32)[:, None] * inv[None, :]
    sin = jnp.sin(ang)[None, :, None, :]
    cos = jnp.cos(ang)[None, :, None, :]
    t1, t2 = t[..., : d // 2], t[..., d // 2:]
    return jnp.concatenate([t1 * cos - t2 * sin, t2 * cos + t1 * sin], axis=-1)


def _even_mixer(x, w_in, w_gate_up, b_gate, gla_norm_g, conv_w, conv_b, conv_norm_g, conv_norm_b, w_out):
    bsz, l_, _ = x.shape
    f32 = jnp.float32
    h = x @ w_in
    q = h[..., :EV_K].astype(f32).reshape(bsz, l_, GLA_HEADS, GLA_DK) * (GLA_DK ** -0.5)
    k = h[..., EV_K:EV_V].astype(f32).reshape(bsz, l_, GLA_HEADS, GLA_DK)
    v = h[..., EV_V:EV_G].astype(f32).reshape(bsz, l_, GLA_HEADS, GLA_DV)
    g = h[..., EV_G:EV_LR].astype(f32)
    lr = h[..., EV_LR:EV_CV].astype(f32).reshape(bsz, l_, 2, GLA_RANK)
    logits = jnp.einsum('blzr,zrk->blzk', lr, w_gate_up.astype(f32)) + b_gate.astype(f32)
    log_a = jax.nn.log_sigmoid(logits) / GLA_TAU
    la_f = log_a[:, :, 0].reshape(bsz, l_, GLA_HEADS, GLA_DK)
    la_b = log_a[:, :, 1].reshape(bsz, l_, GLA_HEADS, GLA_DK)
    o_f = _gla_scan(q, k, v, la_f)
    o_b = _flip(_gla_scan(_flip(q), _flip(k), _flip(v), _flip(la_b)))
    o = o_f + o_b - jnp.einsum('blhd,blhd->blh', q, k)[..., None] * v
    o = _headnorm(o, gla_norm_g.reshape(GLA_HEADS, GLA_DV)).reshape(bsz, l_, GLA_W)
    out_a = (o * jax.nn.silu(g)).astype(x.dtype)
    cv = h[..., EV_CV:]
    c = cv[..., :CONV_CH] * jax.nn.sigmoid(cv[..., CONV_CH:])
    c = lax.conv_general_dilated(c, conv_w.astype(c.dtype)[:, None, :], window_strides=(1,),
                                 padding=[(CONV_WIDTH // 2, CONV_WIDTH // 2)],
                                 dimension_numbers=('NWC', 'WIO', 'NWC'),
                                 feature_group_count=CONV_CH) + conv_b.astype(c.dtype)
    out_b = jax.nn.silu(_layernorm(c, conv_norm_g, conv_norm_b)).astype(x.dtype)
    return jnp.concatenate([out_a, out_b], axis=-1) @ w_out


def _odd_mixer(x, w_in, ret_norm_g, sgu_norm_g, sgu_norm_b, w_spatial, b_spatial, w_out):
    bsz, l_, _ = x.shape
    f32 = jnp.float32
    h = x @ w_in
    q = _rotary(h[..., :OD_K].astype(f32).reshape(bsz, l_, RET_HEADS, RET_DK))
    k = _rotary(h[..., OD_K:OD_V].astype(f32).reshape(bsz, l_, RET_HEADS, RET_DK)) * (RET_DK ** -0.5)
    v = h[..., OD_V:OD_G].astype(f32).reshape(bsz, l_, RET_HEADS, RET_DV)
    g = h[..., OD_G:OD_S].astype(f32)
    hh = jnp.arange(RET_HEADS, dtype=f32)
    log_g_f = jnp.log1p(-jnp.exp2(-5.0 - 2.0 * hh))
    log_g_b = jnp.log1p(-jnp.exp2(-6.0 - 2.0 * hh))
    o_f = _ret_scan(q, k, v, log_g_f)
    o_b = _flip(_ret_scan(_flip(q), _flip(k), _flip(v), log_g_b))
    o = o_f + o_b - jnp.einsum('blhd,blhd->blh', q, k)[..., None] * v
    o = _headnorm(o, ret_norm_g.reshape(RET_HEADS, RET_DV)).reshape(bsz, l_, RET_W)
    out_c = (o * jax.nn.silu(g)).astype(x.dtype)
    s = jax.nn.gelu(h[..., OD_S:], approximate=False)
    u, z = s[..., :SGU_CH], s[..., SGU_CH:]
    z = _layernorm(z, sgu_norm_g, sgu_norm_b).astype(x.dtype)
    z = z.reshape(bsz, l_ // SGU_CHUNK, SGU_CHUNK, SGU_GROUPS, SGU_CH // SGU_GROUPS)
    sp = jnp.einsum('gij,bnjgc->bnigc', w_spatial.astype(z.dtype), z) + b_spatial.T.astype(z.dtype)[None, None, :, :, None]
    out_d = u * sp.reshape(bsz, l_, SGU_CH)
    return jnp.concatenate([out_c, out_d], axis=-1) @ w_out


def _expert_choice(x, w_router, w_gate, w_up, w_down):
    bsz, l_, dm = x.shape
    xt = x.reshape(bsz * l_, dm)
    n_tok = bsz * l_
    cap = CAP_FACTOR * n_tok // N_EXPERTS
    probs = jax.nn.softmax((xt @ w_router).astype(jnp.float32), axis=-1)
    gate, idx = lax.top_k(probs.T, cap)
    xe = jnp.take(xt, idx, axis=0)
    hid = jax.nn.silu(jnp.einsum('ecd,edf->ecf', xe, w_gate)) * jnp.einsum('ecd,edf->ecf', xe, w_up)
    ye = jnp.einsum('ecf,efd->ecd', hid, w_down) * gate[..., None].astype(x.dtype)
    out = jnp.zeros_like(xt).at[idx.reshape(-1)].add(ye.reshape(-1, dm))
    return out.reshape(bsz, l_, dm)


def _trunk(x, ev_w_in, ev_w_gate_up, ev_b_gate, ev_gla_norm_g, ev_conv_w, ev_conv_b, ev_conv_norm_g,
           ev_conv_norm_b, ev_w_out, od_w_in, od_ret_norm_g, od_sgu_norm_g, od_sgu_norm_b, od_w_spatial,
           od_b_spatial, od_w_out, ln_mix_g, ln_mix_b, ln_ffn_g, ln_ffn_b, w_router, w_gate, w_up, w_down):
    for layer in range(DEPTH):
        i = layer // 2
        if layer % 2 == 0:
            m = _even_mixer(x, ev_w_in[i], ev_w_gate_up[i], ev_b_gate[i], ev_gla_norm_g[i], ev_conv_w[i],
                            ev_conv_b[i], ev_conv_norm_g[i], ev_conv_norm_b[i], ev_w_out[i])
        else:
            m = _odd_mixer(x, od_w_in[i], od_ret_norm_g[i], od_sgu_norm_g[i], od_sgu_norm_b[i],
                           od_w_spatial[i], od_b_spatial[i], od_w_out[i])
        x = _layernorm(ALPHA * x + m, ln_mix_g[layer], ln_mix_b[layer]).astype(x.dtype)
        f = _expert_choice(x, w_router[layer], w_gate[layer], w_up[layer], w_down[layer])
        x = _layernorm(ALPHA * x + f, ln_ffn_g[layer], ln_ffn_b[layer]).astype(x.dtype)
    return x


def setup_inputs(seed: int = 0) -> dict:
    key = jax.random.key(seed)
    ks = jax.random.split(key, 32)
    f32 = jnp.float32

    def nrm(k, shape, scale):
        return jax.random.normal(k, shape, f32) * scale

    d = D_MODEL
    return {
        'x_prompt': nrm(ks[0], (BATCH, SEQ, d), 1.0),
        'x_sample': nrm(ks[1], (DEC_BATCH, DEC_SEQ, d), 1.0),
        'ev_w_in': nrm(ks[2], (N_EVEN, d, EVEN_COLS), d ** -0.5),
        'ev_w_gate_up': nrm(ks[3], (N_EVEN, 2, GLA_RANK, GLA_QK), GLA_RANK ** -0.5),
        'ev_b_gate': nrm(ks[4], (N_EVEN, 2, GLA_QK), 0.1),
        'ev_gla_norm_g': 1.0 + nrm(ks[5], (N_EVEN, GLA_W), 0.02),
        'ev_conv_w': nrm(ks[6], (N_EVEN, CONV_WIDTH, CONV_CH), CONV_WIDTH ** -0.5),
        'ev_conv_b': nrm(ks[7], (N_EVEN, CONV_CH), 0.02),
        'ev_conv_norm_g': 1.0 + nrm(ks[8], (N_EVEN, CONV_CH), 0.02),
        'ev_conv_norm_b': nrm(ks[9], (N_EVEN, CONV_CH), 0.02),
        'ev_w_out': nrm(ks[10], (N_EVEN, MIX_W, d), (MIX_W ** -0.5) * BETA),
        'od_w_in': nrm(ks[11], (N_ODD, d, ODD_COLS), d ** -0.5),
        'od_ret_norm_g': 1.0 + nrm(ks[12], (N_ODD, RET_W), 0.02),
        'od_sgu_norm_g': 1.0 + nrm(ks[13], (N_ODD, SGU_CH), 0.02),
        'od_sgu_norm_b': nrm(ks[14], (N_ODD, SGU_CH), 0.02),
        'od_w_spatial': nrm(ks[15], (N_ODD, SGU_GROUPS, SGU_CHUNK, SGU_CHUNK), SGU_CHUNK ** -0.5),
        'od_b_spatial': 1.0 + nrm(ks[16], (N_ODD, SGU_GROUPS, SGU_CHUNK), 0.02),
        'od_w_out': nrm(ks[17], (N_ODD, MIX_W, d), (MIX_W ** -0.5) * BETA),
        'ln_mix_g': 1.0 + nrm(ks[18], (DEPTH, d), 0.02),
        'ln_mix_b': nrm(ks[19], (DEPTH, d), 0.02),
        'ln_ffn_g': 1.0 + nrm(ks[20], (DEPTH, d), 0.02),
        'ln_ffn_b': nrm(ks[21], (DEPTH, d), 0.02),
        'w_router': nrm(ks[22], (DEPTH, d, N_EXPERTS), d ** -0.5),
        'w_gate': nrm(ks[23], (DEPTH, N_EXPERTS, d, D_EXPERT), d ** -0.5),
        'w_up': nrm(ks[24], (DEPTH, N_EXPERTS, d, D_EXPERT), d ** -0.5),
        'w_down': nrm(ks[25], (DEPTH, N_EXPERTS, D_EXPERT, d), (D_EXPERT ** -0.5) * BETA),
    }


def reference(x_prompt, x_sample, ev_w_in, ev_w_gate_up, ev_b_gate, ev_gla_norm_g, ev_conv_w, ev_conv_b,
              ev_conv_norm_g, ev_conv_norm_b, ev_w_out, od_w_in, od_ret_norm_g, od_sgu_norm_g, od_sgu_norm_b,
              od_w_spatial, od_b_spatial, od_w_out, ln_mix_g, ln_mix_b, ln_ffn_g, ln_ffn_b, w_router, w_gate,
              w_up, w_down):
    y_prompt = _trunk(x_prompt, ev_w_in, ev_w_gate_up, ev_b_gate, ev_gla_norm_g, ev_conv_w, ev_conv_b,
                      ev_conv_norm_g, ev_conv_norm_b, ev_w_out, od_w_in, od_ret_norm_g, od_sgu_norm_g,
                      od_sgu_norm_b, od_w_spatial, od_b_spatial, od_w_out, ln_mix_g, ln_mix_b, ln_ffn_g,
                      ln_ffn_b, w_router, w_gate, w_up, w_down)
    y_sample = _trunk(x_sample, ev_w_in, ev_w_gate_up, ev_b_gate, ev_gla_norm_g, ev_conv_w, ev_conv_b,
                      ev_conv_norm_g, ev_conv_norm_b, ev_w_out, od_w_in, od_ret_norm_g, od_sgu_norm_g,
                      od_sgu_norm_b, od_w_spatial, od_b_spatial, od_w_out, ln_mix_g, ln_mix_b, ln_ffn_g,
                      ln_ffn_b, w_router, w_gate, w_up, w_down)
    return (y_prompt, y_sample)
```

```python
import functools

import jax
import jax.numpy as jnp
from jax import lax
from jax.experimental import pallas as pl
from jax.experimental.pallas import tpu as pltpu

F32 = jnp.float32
BF16 = jnp.bfloat16
I32 = jnp.int32

D_MODEL = 2048
DEPTH = 4
GLA_HEADS = 4
GLA_DV = 256
GLA_DK = 128
GLA_QK = GLA_HEADS * GLA_DK
GLA_W = GLA_HEADS * GLA_DV
GLA_RANK = 16
GLA_TAU = 16.0
GLA_CHUNK = 64
GLA_SUB = 16
CONV_CH = 1024
CONV_WIDTH = 31
RET_HEADS = 4
RET_DK = 256
RET_DV = 256
RET_CHUNK = 128
SGU_CH = 1024
SGU_GROUPS = 4
SGU_CHUNK = 128
N_EXPERTS = 16
CAP_FACTOR = 2
D_EXPERT = 2816
ALPHA = (2 * DEPTH) ** 0.25
LN_EPS = 1e-5

LANES = 128
VMEM_LIMIT = 56 * 1024 * 1024

NT_DIMS = (((1,), (1,)), ((), ()))
TN_DIMS = (((0,), (0,)), ((), ()))


def _params(*sem):
    return pltpu.CompilerParams(dimension_semantics=sem, vmem_limit_bytes=VMEM_LIMIT)


def _dot(a, b):
    return jnp.dot(a, b, preferred_element_type=F32)


def _sigmoid(x):
    return 1.0 / (1.0 + jnp.exp(-x))


def _ln_rows(v, g, b):
    mu = jnp.mean(v, axis=-1, keepdims=True)
    d = v - mu
    var = jnp.mean(d * d, axis=-1, keepdims=True)
    return d * lax.rsqrt(var + LN_EPS) * g + b


def _mm_kernel(x_ref, w_ref, o_ref):
    o_ref[...] = _dot(x_ref[...].astype(BF16), w_ref[...])


def _matmul(x, w, bm, bn, name):
    m, k = x.shape
    n = w.shape[1]
    return pl.pallas_call(
        _mm_kernel,
        grid=(m // bm, n // bn),
        in_specs=[pl.BlockSpec((bm, k), lambda i, j: (i, 0)),
                  pl.BlockSpec((k, bn), lambda i, j: (0, j))],
        out_specs=pl.BlockSpec((bm, bn), lambda i, j: (i, j)),
        out_shape=jax.ShapeDtypeStruct((m, n), F32),
        compiler_params=_params("parallel", "arbitrary"),
        name=name,
    )(x, w)


GLA_TB = 256


def _gla_kernel(rev, *refs):
    if rev:
        (q_ref, k_ref, v_ref, lr_ref, wz_ref, bz_ref, tri_ref, of_ref, g_ref, ng_ref,
         out_ref, st_ref, cum_s, q_s, k_s, v_s, o_s) = refs
    else:
        (q_ref, k_ref, v_ref, lr_ref, wz_ref, bz_ref, tri_ref,
         out_ref, st_ref, cum_s, q_s, k_s, v_s, o_s) = refs
    c, s = GLA_CHUNK, GLA_SUB
    nsub = c // s
    nch = GLA_TB // c

    @pl.when(pl.program_id(1) == 0)
    def _():
        st_ref[...] = jnp.zeros_like(st_ref)

    def chunk(ci, carry):
        cc = (nch - 1 - ci) if rev else ci
        r0 = pl.multiple_of(cc * c, c)
        q = q_ref[pl.ds(r0, c), :] * (GLA_DK ** -0.5)
        k = k_ref[pl.ds(r0, c), :]
        v = v_ref[pl.ds(r0, c), :]
        logits = _dot(lr_ref[pl.ds(r0, c), :].astype(BF16), wz_ref[...]) + bz_ref[...]
        la = (jnp.minimum(logits, 0.0) - jnp.log1p(jnp.exp(-jnp.abs(logits)))) * (1.0 / GLA_TAU)
        cum = jnp.dot(tri_ref[...], la, precision=lax.Precision.HIGHEST, preferred_element_type=F32)
        tot = cum[0:1] if rev else cum[c - 1:c]
        qe = (q * jnp.exp(cum)).astype(BF16)
        kd = (k * jnp.exp(tot - cum)).astype(BF16)
        vb = v.astype(BF16)
        cum_s[...] = cum
        q_s[...] = q
        k_s[...] = k
        v_s[...] = v
        for h in range(GLA_HEADS):
            hk = slice(h * GLA_DK, (h + 1) * GLA_DK)
            hv = slice(h * GLA_DV, (h + 1) * GLA_DV)
            o_s[:, hv] = lax.dot_general(qe[:, hk], st_ref[h].astype(BF16), NT_DIMS,
                                         preferred_element_type=F32)
        for blk in range(nsub):
            rows = slice(blk * s, (blk + 1) * s)
            if rev:
                if blk == nsub - 1:
                    continue
                bidx, js = (blk + 1) * s - 1, slice((blk + 1) * s, c)
            else:
                if blk == 0:
                    continue
                bidx, js = blk * s, slice(0, blk * s)
            b = cum[bidx:bidx + 1]
            qt = (q[rows] * jnp.exp(cum[rows] - b)).astype(BF16)
            kt = (k[js] * jnp.exp(b - cum[js])).astype(BF16)
            for h in range(GLA_HEADS):
                hk = slice(h * GLA_DK, (h + 1) * GLA_DK)
                hv = slice(h * GLA_DV, (h + 1) * GLA_DV)
                att = lax.dot_general(qt[:, hk], kt[:, hk], NT_DIMS, preferred_element_type=F32)
                o_s[rows, hv] += _dot(att.astype(BF16), vb[js, hv])

        def pair(j, carry2):
            base = pl.multiple_of(lax.shift_left(lax.shift_right_logical(j, 4), 4), s)
            cj = cum_s[pl.ds(j, 1), :]
            kj = k_s[pl.ds(j, 1), :]
            vj = v_s[pl.ds(j, 1), :]
            ci_ = cum_s[pl.ds(base, s), :]
            qi = q_s[pl.ds(base, s), :]
            ridx = base + lax.broadcasted_iota(I32, (s, 1), 0)
            m = (ridx <= j) if rev else (ridx > j)
            p = qi * kj * jnp.exp(jnp.where(m, ci_ - cj, -jnp.inf))
            for h in range(GLA_HEADS):
                hk = slice(h * GLA_DK, (h + 1) * GLA_DK)
                hv = slice(h * GLA_DV, (h + 1) * GLA_DV)
                col = jnp.sum(p[:, hk], axis=-1, keepdims=True)
                o_s[pl.ds(base, s), hv] += col * vj[:, hv]
            return carry2

        lax.fori_loop(0, c, pair, 0)

        for h in range(GLA_HEADS):
            hk = slice(h * GLA_DK, (h + 1) * GLA_DK)
            hv = slice(h * GLA_DV, (h + 1) * GLA_DV)
            upd = lax.dot_general(vb[:, hv], kd[:, hk], TN_DIMS, preferred_element_type=F32)
            st_ref[h] = st_ref[h] * jnp.exp(tot[:, hk]) + upd

        if rev:
            o = of_ref[pl.ds(r0, c), :] + o_s[...]
            g = g_ref[pl.ds(r0, c), :]
            for h in range(GLA_HEADS):
                hv = slice(h * GLA_DV, (h + 1) * GLA_DV)
                oh = o[:, hv]
                mu = jnp.mean(oh, axis=-1, keepdims=True)
                d = oh - mu
                var = jnp.mean(d * d, axis=-1, keepdims=True)
                y = d * lax.rsqrt(var + LN_EPS) * ng_ref[:, hv]
                gh = g[:, hv]
                out_ref[pl.ds(r0, c), hv] = (y * (gh * _sigmoid(gh))).astype(out_ref.dtype)
        else:
            out_ref[pl.ds(r0, c), :] = o_s[...]
        return carry

    lax.fori_loop(0, nch, chunk, 0)


def _gla(h, lr, wz, bz, tri, nb, seqlen, rev, of=None, norm_g=None):
    n = nb * seqlen
    nblk = seqlen // GLA_TB

    def blk(b, c):
        return b * nblk + ((nblk - 1 - c) if rev else c)

    in_specs = [
        pl.BlockSpec((GLA_TB, GLA_QK), lambda b, c: (blk(b, c), 0)),
        pl.BlockSpec((GLA_TB, GLA_QK), lambda b, c: (blk(b, c), 1)),
        pl.BlockSpec((GLA_TB, GLA_W), lambda b, c: (blk(b, c), 1)),
        pl.BlockSpec((GLA_TB, LANES), lambda b, c: (blk(b, c), 0)),
        pl.BlockSpec((LANES, GLA_QK), lambda b, c: (0, 0)),
        pl.BlockSpec((1, GLA_QK), lambda b, c: (0, 0)),
        pl.BlockSpec((GLA_CHUNK, GLA_CHUNK), lambda b, c: (0, 0)),
    ]
    args = [h, h, h, lr, wz, bz, tri]
    if rev:
        in_specs += [
            pl.BlockSpec((GLA_TB, GLA_W), lambda b, c: (blk(b, c), 0)),
            pl.BlockSpec((GLA_TB, GLA_W), lambda b, c: (blk(b, c), 2)),
            pl.BlockSpec((1, GLA_W), lambda b, c: (0, 0)),
        ]
        args += [of, h, norm_g]
    return pl.pallas_call(
        functools.partial(_gla_kernel, rev),
        grid=(nb, nblk),
        in_specs=in_specs,
        out_specs=pl.BlockSpec((GLA_TB, GLA_W), lambda b, c: (blk(b, c), 0)),
        out_shape=jax.ShapeDtypeStruct((n, GLA_W), BF16 if rev else F32),
        scratch_shapes=[
            pltpu.VMEM((GLA_HEADS, GLA_DV, GLA_DK), F32),
            pltpu.VMEM((GLA_CHUNK, GLA_QK), F32),
            pltpu.VMEM((GLA_CHUNK, GLA_QK), F32),
            pltpu.VMEM((GLA_CHUNK, GLA_QK), F32),
            pltpu.VMEM((GLA_CHUNK, GLA_W), F32),
            pltpu.VMEM((GLA_CHUNK, GLA_W), F32),
        ],
        compiler_params=_params("parallel", "arbitrary"),
        name="gla_bwd" if rev else "gla_fwd",
    )(*args)


CONV_TB = 512
CONV_HALO = 16
CONV_RT = 16


def _conv_kernel(a_ref, g_ref, ap_ref, gp_ref, an_ref, gn_ref, cw_ref, cb_ref, lg_ref, lb_ref,
                 out_ref, ext_s):
    t, hl = CONV_TB, CONV_HALO
    c = pl.program_id(1)
    last = pl.num_programs(1) - 1
    ext_s[hl:hl + t, :] = a_ref[...] * _sigmoid(g_ref[...])
    prev = ap_ref[...] * _sigmoid(gp_ref[...])
    nxt = an_ref[...] * _sigmoid(gn_ref[...])
    ext_s[0:hl, :] = jnp.where(c > 0, prev, 0.0)
    ext_s[hl + t:hl + t + hl, :] = jnp.where(c < last, nxt, 0.0)
    off = hl - CONV_WIDTH // 2
    win_rows = CONV_RT + 2 * hl
    sub = 8

    def tile(i, carry):
        r0 = pl.multiple_of(i * CONV_RT, CONV_RT)
        win = ext_s[pl.ds(r0, win_rows), :]
        acc = jnp.zeros((CONV_RT, CONV_CH), F32)
        for ph in range(sub):
            rolled = win if ph == 0 else pltpu.roll(win, win_rows - ph, axis=0)
            for w in range(CONV_WIDTH):
                if (off + w) % sub == ph:
                    a0 = off + w - ph
                    acc = acc + rolled[a0:a0 + CONV_RT, :] * cw_ref[w:w + 1, :]
        y = _ln_rows(acc + cb_ref[...], lg_ref[...], lb_ref[...])
        out_ref[pl.ds(r0, CONV_RT), :] = (y * _sigmoid(y)).astype(out_ref.dtype)
        return carry

    lax.fori_loop(0, t // CONV_RT, tile, 0)


def _conv(h, cw, cb, lg, lb, nb, seqlen, col0):
    n = nb * seqlen
    nblk = seqlen // CONV_TB
    per = CONV_TB // CONV_HALO
    nhalo = n // CONV_HALO

    def cur(b, c):
        return b * nblk + c

    def prv(b, c):
        return jnp.maximum(cur(b, c) * per - 1, 0)

    def nxt(b, c):
        return jnp.minimum((cur(b, c) + 1) * per, nhalo - 1)

    row = lambda b, c: (0, 0)
    return pl.pallas_call(
        _conv_kernel,
        grid=(nb, nblk),
        in_specs=[
            pl.BlockSpec((CONV_TB, CONV_CH), lambda b, c: (cur(b, c), col0)),
            pl.BlockSpec((CONV_TB, CONV_CH), lambda b, c: (cur(b, c), col0 + 1)),
            pl.BlockSpec((CONV_HALO, CONV_CH), lambda b, c: (prv(b, c), col0)),
            pl.BlockSpec((CONV_HALO, CONV_CH), lambda b, c: (prv(b, c), col0 + 1)),
            pl.BlockSpec((CONV_HALO, CONV_CH), lambda b, c: (nxt(b, c), col0)),
            pl.BlockSpec((CONV_HALO, CONV_CH), lambda b, c: (nxt(b, c), col0 + 1)),
            pl.BlockSpec((CONV_WIDTH + 1, CONV_CH), row),
            pl.BlockSpec((1, CONV_CH), row),
            pl.BlockSpec((1, CONV_CH), row),
            pl.BlockSpec((1, CONV_CH), row),
        ],
        out_specs=pl.BlockSpec((CONV_TB, CONV_CH), lambda b, c: (cur(b, c), 0)),
        out_shape=jax.ShapeDtypeStruct((n, CONV_CH), BF16),
        scratch_shapes=[pltpu.VMEM((CONV_TB + 2 * CONV_HALO, CONV_CH), F32)],
        compiler_params=_params("parallel", "arbitrary"),
        name="conv_module",
    )(h, h, h, h, h, h, cw, cb, lg, lb)


def _ret_kernel(rev, *refs):
    if rev:
        (q_ref, k_ref, v_ref, cos_ref, sin_ref, dm_ref, qd_ref, kd_ref, cd_ref, of_ref, g_ref, ng_ref,
         out_ref, st_ref) = refs
    else:
        (q_ref, k_ref, v_ref, cos_ref, sin_ref, dm_ref, qd_ref, kd_ref, cd_ref,
         out_ref, st_ref) = refs
    half = RET_DK // 2

    @pl.when(pl.program_id(1) == 0)
    def _():
        st_ref[...] = jnp.zeros_like(st_ref)

    cos = cos_ref[...]
    sin = sin_ref[...]
    for h in range(RET_HEADS):
        lo = slice(h * RET_DK, h * RET_DK + half)
        hi = slice(h * RET_DK + half, (h + 1) * RET_DK)
        hv = slice(h * RET_DV, (h + 1) * RET_DV)
        q1, q2 = q_ref[:, lo], q_ref[:, hi]
        k1, k2 = k_ref[:, lo], k_ref[:, hi]
        qr = jnp.concatenate([q1 * cos - q2 * sin, q2 * cos + q1 * sin], axis=-1)
        kr = jnp.concatenate([k1 * cos - k2 * sin, k2 * cos + k1 * sin], axis=-1) * (RET_DK ** -0.5)
        qb = qr.astype(BF16)
        vb = v_ref[:, hv].astype(BF16)
        att = lax.dot_general(qb, kr.astype(BF16), NT_DIMS, preferred_element_type=F32) * dm_ref[h]
        o = _dot(att.astype(BF16), vb) + _dot(qb, st_ref[h].astype(BF16)) * qd_ref[h]
        upd = lax.dot_general((kr * kd_ref[h]).astype(BF16), vb, TN_DIMS, preferred_element_type=F32)
        st_ref[h] = st_ref[h] * cd_ref[h] + upd
        if rev:
            o = of_ref[:, hv] + o
            mu = jnp.mean(o, axis=-1, keepdims=True)
            d = o - mu
            var = jnp.mean(d * d, axis=-1, keepdims=True)
            y = d * lax.rsqrt(var + LN_EPS) * ng_ref[:, hv]
            gh = g_ref[:, hv]
            out_ref[:, hv] = (y * (gh * _sigmoid(gh))).astype(out_ref.dtype)
        else:
            out_ref[:, hv] = o


def _ret(h, cos, sin, dm, qd, kd, cd, nb, seqlen, rev, of=None, norm_g=None):
    n = nb * seqlen
    nblk = seqlen // RET_CHUNK
    w = RET_HEADS * RET_DK

    def pos(b, c):
        return (nblk - 1 - c) if rev else c

    def blk(b, c):
        return b * nblk + pos(b, c)

    const3 = lambda b, c: (0, 0, 0)
    in_specs = [
        pl.BlockSpec((RET_CHUNK, w), lambda b, c: (blk(b, c), 0)),
        pl.BlockSpec((RET_CHUNK, w), lambda b, c: (blk(b, c), 1)),
        pl.BlockSpec((RET_CHUNK, w), lambda b, c: (blk(b, c), 2)),
        pl.BlockSpec((RET_CHUNK, RET_DK // 2), lambda b, c: (pos(b, c), 0)),
        pl.BlockSpec((RET_CHUNK, RET_DK // 2), lambda b, c: (pos(b, c), 0)),
        pl.BlockSpec((RET_HEADS, RET_CHUNK, RET_CHUNK), const3),
        pl.BlockSpec((RET_HEADS, RET_CHUNK, 1), const3),
        pl.BlockSpec((RET_HEADS, RET_CHUNK, 1), const3),
        pl.BlockSpec((RET_HEADS, 1, RET_DV), const3),
    ]
    args = [h, h, h, cos, sin, dm, qd, kd, cd]
    if rev:
        in_specs += [
            pl.BlockSpec((RET_CHUNK, w), lambda b, c: (blk(b, c), 0)),
            pl.BlockSpec((RET_CHUNK, w), lambda b, c: (blk(b, c), 3)),
            pl.BlockSpec((1, w), lambda b, c: (0, 0)),
        ]
        args += [of, h, norm_g]
    return pl.pallas_call(
        functools.partial(_ret_kernel, rev),
        grid=(nb, nblk),
        in_specs=in_specs,
        out_specs=pl.BlockSpec((RET_CHUNK, w), lambda b, c: (blk(b, c), 0)),
        out_shape=jax.ShapeDtypeStruct((n, w), BF16 if rev else F32),
        scratch_shapes=[pltpu.VMEM((RET_HEADS, RET_DK, RET_DV), F32)],
        compiler_params=_params("parallel", "arbitrary"),
        name="ret_bwd" if rev else "ret_fwd",
    )(*args)


def _ret_consts(log_g, rev):
    c = RET_CHUNK
    idx = jnp.arange(c, dtype=F32)
    lg = log_g[:, None]
    if rev:
        diff = idx[None, :] - idx[:, None]
        keep = diff >= 0
        qd = jnp.exp(lg * (c - idx))
        kd = jnp.exp(lg * idx)
    else:
        diff = idx[:, None] - idx[None, :]
        keep = diff > 0
        qd = jnp.exp(lg * (idx + 1.0))
        kd = jnp.exp(lg * (c - 1.0 - idx))
    dm = jnp.where(keep[None], jnp.exp(log_g[:, None, None] * jnp.maximum(diff, 0.0)[None]), 0.0)
    cd = jnp.broadcast_to(jnp.exp(log_g * c)[:, None, None], (RET_HEADS, 1, RET_DV))
    return dm, qd[:, :, None], kd[:, :, None], cd


SGU_TB = 512


def _gelu(x):
    return 0.5 * x * (1.0 + lax.erf(x * (2.0 ** -0.5)))


def _sgu_kernel(u_ref, z_ref, lg_ref, lb_ref, ws_ref, bs_ref, out_ref):
    gw = SGU_CH // SGU_GROUPS
    for ci in range(SGU_TB // SGU_CHUNK):
        rows = slice(ci * SGU_CHUNK, (ci + 1) * SGU_CHUNK)
        u = _gelu(u_ref[rows, :])
        zn = _ln_rows(_gelu(z_ref[rows, :]), lg_ref[...], lb_ref[...]).astype(BF16)
        for g in range(SGU_GROUPS):
            cols = slice(g * gw, (g + 1) * gw)
            sp = _dot(ws_ref[g], zn[:, cols]) + bs_ref[g]
            out_ref[rows, cols] = (u[:, cols] * sp).astype(out_ref.dtype)


def _sgu(h, lg, lb, ws, bs, n, col0):
    const3 = lambda i: (0, 0, 0)
    return pl.pallas_call(
        _sgu_kernel,
        grid=(n // SGU_TB,),
        in_specs=[
            pl.BlockSpec((SGU_TB, SGU_CH), lambda i: (i, col0)),
            pl.BlockSpec((SGU_TB, SGU_CH), lambda i: (i, col0 + 1)),
            pl.BlockSpec((1, SGU_CH), lambda i: (0, 0)),
            pl.BlockSpec((1, SGU_CH), lambda i: (0, 0)),
            pl.BlockSpec((SGU_GROUPS, SGU_CHUNK, SGU_CHUNK), const3),
            pl.BlockSpec((SGU_GROUPS, SGU_CHUNK, 1), const3),
        ],
        out_specs=pl.BlockSpec((SGU_TB, SGU_CH), lambda i: (i, 0)),
        out_shape=jax.ShapeDtypeStruct((n, SGU_CH), BF16),
        compiler_params=_params("parallel"),
        name="sgu",
    )(h, h, lg, lb, ws, bs)


OUT_BM = 256


def _outproj_kernel(a_ref, b_ref, w_ref, x_ref, g_ref, bb_ref, wr_ref, x1_ref, p_ref):
    half = D_MODEL // 2
    m = _dot(a_ref[...], w_ref[0:half, :]) + _dot(b_ref[...], w_ref[half:D_MODEL, :])
    x1 = _ln_rows(ALPHA * x_ref[...] + m, g_ref[...], bb_ref[...])
    x1_ref[...] = x1
    logits = lax.dot_general(wr_ref[...], x1, NT_DIMS, precision=lax.Precision.HIGHEST,
                             preferred_element_type=F32)
    e = jnp.exp(logits - jnp.max(logits, axis=0, keepdims=True))
    p_ref[...] = e / jnp.sum(e, axis=0, keepdims=True)


def _outproj(a, b, w, x, g, bb, wr_t):
    n = x.shape[0]
    half = D_MODEL // 2
    row = lambda i: (0, 0)
    return pl.pallas_call(
        _outproj_kernel,
        grid=(n // OUT_BM,),
        in_specs=[
            pl.BlockSpec((OUT_BM, half), lambda i: (i, 0)),
            pl.BlockSpec((OUT_BM, half), lambda i: (i, 0)),
            pl.BlockSpec((D_MODEL, D_MODEL), row),
            pl.BlockSpec((OUT_BM, D_MODEL), lambda i: (i, 0)),
            pl.BlockSpec((1, D_MODEL), row),
            pl.BlockSpec((1, D_MODEL), row),
            pl.BlockSpec((N_EXPERTS, D_MODEL), row),
        ],
        out_specs=[pl.BlockSpec((OUT_BM, D_MODEL), lambda i: (i, 0)),
                   pl.BlockSpec((N_EXPERTS, OUT_BM), lambda i: (0, i))],
        out_shape=[jax.ShapeDtypeStruct((n, D_MODEL), F32),
                   jax.ShapeDtypeStruct((N_EXPERTS, n), F32)],
        compiler_params=_params("parallel"),
        name="outproj_ln_router",
    )(a, b, w, x, g, bb, wr_t)


def _select_kernel(cap, p_ref, sel_ref, pos_ref):
    n = p_ref.shape[1]
    bits = pltpu.bitcast(p_ref[...], I32)

    def search(i, thr):
        cand = thr | lax.shift_left(jnp.int32(1), 30 - i)
        cnt = jnp.sum((bits >= cand).astype(I32), axis=1, keepdims=True)
        return jnp.where(cnt >= cap, cand, thr)

    thr = lax.fori_loop(0, 31, search, jnp.zeros((N_EXPERTS, 1), I32))
    need = cap - jnp.sum((bits > thr).astype(I32), axis=1, keepdims=True)
    need_f = need.astype(F32)
    r = lax.broadcasted_iota(I32, (LANES, LANES), 0)
    cidx = lax.broadcasted_iota(I32, (LANES, LANES), 1)
    upper = (r <= cidx).astype(BF16)

    def block(i, carry):
        eq_run, sel_run = carry
        off = pl.multiple_of(i * LANES, LANES)
        b = pltpu.bitcast(p_ref[:, pl.ds(off, LANES)], I32)
        eq = (b == thr).astype(F32)
        eq_before = eq_run + _dot(eq.astype(BF16), upper) - eq
        sel = jnp.where(b > thr, 1.0, jnp.where(eq_before < need_f, eq, 0.0))
        incl = _dot(sel.astype(BF16), upper)
        sel_ref[:, pl.ds(off, LANES)] = sel.astype(I32)
        pos_ref[:, pl.ds(off, LANES)] = (sel_run + incl - sel).astype(I32)
        return (eq_run + jnp.sum(eq, axis=1, keepdims=True),
                sel_run + jnp.sum(sel, axis=1, keepdims=True))

    zero = jnp.zeros((N_EXPERTS, 1), F32)
    lax.fori_loop(0, n // LANES, block, (zero, zero))


def _select(probs_t, cap):
    n = probs_t.shape[1]
    full = pl.BlockSpec((N_EXPERTS, n), lambda i: (0, 0))
    return pl.pallas_call(
        functools.partial(_select_kernel, cap),
        grid=(1,),
        in_specs=[full],
        out_specs=[full, full],
        out_shape=[jax.ShapeDtypeStruct((N_EXPERTS, n), I32)] * 2,
        compiler_params=_params("arbitrary"),
        name="expert_select",
    )(probs_t)


ROWS = 256
UP_BM = 512
UP_FT = D_EXPERT // 2
DOWN_BM = 512


def _row_copy(src_hbm, dst, idx_ref, sem, gather):
    def issue(r, carry):
        t = idx_ref[0, r]
        if gather:
            pltpu.make_async_copy(src_hbm.at[pl.ds(t, 1)], dst.at[pl.ds(r, 1)], sem).start()
        else:
            pltpu.make_async_copy(dst.at[pl.ds(r, 1)], src_hbm.at[pl.ds(t, 1)], sem).start()
        return carry

    lax.fori_loop(0, ROWS, issue, 0)


def _wait_rows(hbm, buf, sem, gather):
    if gather:
        pltpu.make_async_copy(hbm.at[pl.ds(0, ROWS)], buf, sem).wait()
    else:
        pltpu.make_async_copy(buf, hbm.at[pl.ds(0, ROWS)], sem).wait()


def _gather_kernel(idx_ref, x_hbm, out_ref, buf, sem):
    _row_copy(x_hbm, buf, idx_ref, sem, True)
    _wait_rows(x_hbm, buf, sem, True)
    out_ref[...] = buf[...].astype(out_ref.dtype)


def _gather_rows(idx3, x):
    nsteps = idx3.shape[0]
    return pl.pallas_call(
        _gather_kernel,
        grid=(nsteps,),
        in_specs=[pl.BlockSpec((None, 1, ROWS), lambda i: (i, 0, 0), memory_space=pltpu.SMEM),
                  pl.BlockSpec(memory_space=pl.ANY)],
        out_specs=pl.BlockSpec((ROWS, D_MODEL), lambda i: (i, 0)),
        out_shape=jax.ShapeDtypeStruct((nsteps * ROWS, D_MODEL), BF16),
        scratch_shapes=[pltpu.VMEM((ROWS, D_MODEL), F32), pltpu.SemaphoreType.DMA],
        compiler_params=_params("arbitrary"),
        name="moe_gather",
    )(idx3, x)


def _up_kernel(xe_ref, wg_ref, wu_ref, h_ref):
    xe = xe_ref[...]
    a = _dot(xe, wg_ref[...])
    h_ref[...] = (a * _sigmoid(a) * _dot(xe, wu_ref[...])).astype(h_ref.dtype)


def _expert_up(xe, wg, wu, cap):
    nbm = cap // UP_BM
    return pl.pallas_call(
        _up_kernel,
        grid=(N_EXPERTS, D_EXPERT // UP_FT, nbm),
        in_specs=[pl.BlockSpec((UP_BM, D_MODEL), lambda e, f, i: (e * nbm + i, 0)),
                  pl.BlockSpec((None, D_MODEL, UP_FT), lambda e, f, i: (e, 0, f)),
                  pl.BlockSpec((None, D_MODEL, UP_FT), lambda e, f, i: (e, 0, f))],
        out_specs=pl.BlockSpec((UP_BM, UP_FT), lambda e, f, i: (e * nbm + i, f)),
        out_shape=jax.ShapeDtypeStruct((N_EXPERTS * cap, D_EXPERT), BF16),
        compiler_params=_params("parallel", "parallel", "arbitrary"),
        name="expert_up",
    )(xe, wg, wu)


def _down_kernel(h_ref, wd_ref, gate_ref, y_ref):
    y_ref[...] = _dot(h_ref[...], wd_ref[...]) * gate_ref[...]


def _expert_down(hid, wd, gate_col, cap):
    nbm = cap // DOWN_BM
    return pl.pallas_call(
        _down_kernel,
        grid=(N_EXPERTS, nbm),
        in_specs=[pl.BlockSpec((DOWN_BM, D_EXPERT), lambda e, i: (e * nbm + i, 0)),
                  pl.BlockSpec((None, D_EXPERT, D_MODEL), lambda e, i: (e, 0, 0)),
                  pl.BlockSpec((DOWN_BM, 1), lambda e, i: (e * nbm + i, 0))],
        out_specs=pl.BlockSpec((DOWN_BM, D_MODEL), lambda e, i: (e * nbm + i, 0)),
        out_shape=jax.ShapeDtypeStruct((N_EXPERTS * cap, D_MODEL), F32),
        compiler_params=_params("parallel", "arbitrary"),
        name="expert_down",
    )(hid, wd, gate_col)


def _scatter_kernel(idx_ref, y_ref, acc_in, acc_hbm, buf, sem_in, sem_out):
    del acc_in
    _row_copy(acc_hbm, buf, idx_ref, sem_in, True)
    _wait_rows(acc_hbm, buf, sem_in, True)
    buf[...] = buf[...] + y_ref[...]
    _row_copy(acc_hbm, buf, idx_ref, sem_out, False)
    _wait_rows(acc_hbm, buf, sem_out, False)


def _scatter_add(idx3, y, acc):
    nsteps = idx3.shape[0]
    return pl.pallas_call(
        _scatter_kernel,
        grid=(nsteps,),
        in_specs=[pl.BlockSpec((None, 1, ROWS), lambda i: (i, 0, 0), memory_space=pltpu.SMEM),
                  pl.BlockSpec((ROWS, D_MODEL), lambda i: (i, 0)),
                  pl.BlockSpec(memory_space=pl.ANY)],
        out_specs=pl.BlockSpec(memory_space=pl.ANY),
        out_shape=jax.ShapeDtypeStruct(acc.shape, F32),
        scratch_shapes=[pltpu.VMEM((ROWS, D_MODEL), F32), pltpu.SemaphoreType.DMA, pltpu.SemaphoreType.DMA],
        input_output_aliases={2: 0},
        compiler_params=_params("arbitrary"),
        name="moe_scatter_add",
    )(idx3, y, acc)


FIN_BM = 512


def _final_kernel(x_ref, f_ref, g_ref, b_ref, o_ref):
    o_ref[...] = _ln_rows(ALPHA * x_ref[...] + f_ref[...], g_ref[...], b_ref[...])


def _final_ln(x, f, g, b):
    n = x.shape[0]
    tile = pl.BlockSpec((FIN_BM, D_MODEL), lambda i: (i, 0))
    row = pl.BlockSpec((1, D_MODEL), lambda i: (0, 0))
    return pl.pallas_call(
        _final_kernel,
        grid=(n // FIN_BM,),
        in_specs=[tile, tile, row, row],
        out_specs=tile,
        out_shape=jax.ShapeDtypeStruct((n, D_MODEL), F32),
        compiler_params=_params("parallel"),
        name="ffn_residual_ln",
    )(x, f, g, b)


def _moe(x1, probs_t, wg, wu, wd):
    n = x1.shape[0]
    cap = CAP_FACTOR * n // N_EXPERTS
    sel, pos = _select(probs_t, cap)
    tok = jnp.broadcast_to(jnp.arange(n, dtype=I32)[None, :], (N_EXPERTS, n))
    slot = jnp.where(sel > 0, pos, cap)
    eid = jnp.broadcast_to(jnp.arange(N_EXPERTS, dtype=I32)[:, None], (N_EXPERTS, n))
    idx = jnp.zeros((N_EXPERTS, cap + 1), I32).at[eid, slot].set(tok)[:, :cap]
    gate = jnp.take_along_axis(probs_t, idx, axis=1)
    idx3 = idx.reshape(N_EXPERTS * cap // ROWS, 1, ROWS)
    xe = _gather_rows(idx3, x1)
    hid = _expert_up(xe, wg, wu, cap)
    y = _expert_down(hid, wd, gate.reshape(N_EXPERTS * cap, 1), cap)
    return _scatter_add(idx3, y, jnp.zeros((n, D_MODEL), F32))


def _rotary_tables(seqlen):
    half = RET_DK // 2
    inv = 1.0 / (10000.0 ** jnp.linspace(0.0, 1.0, half, dtype=F32))
    ang = jnp.arange(seqlen, dtype=F32)[:, None] * inv[None, :]
    return jnp.cos(ang), jnp.sin(ang)


def _tri(rev):
    i = jnp.arange(GLA_CHUNK)
    m = (i[None, :] >= i[:, None]) if rev else (i[None, :] <= i[:, None])
    return m.astype(F32)


def _even_layer(x, p, i, nb, seqlen):
    n = x.shape[0]
    h = _matmul(x, p["ev_w_main"][i], 1024, 1024, "in_proj_even")
    lr = _matmul(x, p["ev_w_lr"][i], 1024, LANES, "in_proj_lowrank")
    of = _gla(h, lr, p["ev_wz"][i, 0], p["ev_bz"][i, 0], _tri(False), nb, seqlen, False)
    out_a = _gla(h, lr, p["ev_wz"][i, 1], p["ev_bz"][i, 1], _tri(True), nb, seqlen, True,
                 of=of, norm_g=p["ev_gla_norm_g"][i])
    out_b = _conv(h, p["ev_conv_w"][i], p["ev_conv_b"][i], p["ev_conv_norm_g"][i], p["ev_conv_norm_b"][i],
                  nb, seqlen, 3)
    del n
    return out_a, out_b, p["ev_w_out"][i]


def _odd_layer(x, p, i, nb, seqlen):
    n = x.shape[0]
    h = _matmul(x, p["od_w_in"][i], 1024, 1024, "in_proj_odd")
    cos, sin = _rotary_tables(seqlen)
    hh = jnp.arange(RET_HEADS, dtype=F32)
    cf = _ret_consts(jnp.log1p(-jnp.exp2(-5.0 - 2.0 * hh)), False)
    cb = _ret_consts(jnp.log1p(-jnp.exp2(-6.0 - 2.0 * hh)), True)
    of = _ret(h, cos, sin, *cf, nb, seqlen, False)
    out_c = _ret(h, cos, sin, *cb, nb, seqlen, True, of=of, norm_g=p["od_ret_norm_g"][i])
    out_d = _sgu(h, p["od_sgu_norm_g"][i], p["od_sgu_norm_b"][i], p["od_w_spatial"][i], p["od_b_spatial"][i], n, 4)
    return out_c, out_d, p["od_w_out"][i]


def _trunk(x3, p):
    nb, seqlen, _ = x3.shape
    x = x3.reshape(nb * seqlen, D_MODEL)
    for layer in range(DEPTH):
        i = layer // 2
        mix = _even_layer if layer % 2 == 0 else _odd_layer
        a, b, w_out = mix(x, p, i, nb, seqlen)
        x1, probs_t = _outproj(a, b, w_out, x, p["ln_mix_g"][layer], p["ln_mix_b"][layer], p["w_router_t"][layer])
        f = _moe(x1, probs_t, p["w_gate"][layer], p["w_up"][layer], p["w_down"][layer])
        x = _final_ln(x1, f, p["ln_ffn_g"][layer], p["ln_ffn_b"][layer])
    return x.reshape(nb, seqlen, D_MODEL)


def kernel(x_prompt, x_sample, ev_w_in, ev_w_gate_up, ev_b_gate, ev_gla_norm_g, ev_conv_w, ev_conv_b, ev_conv_norm_g, ev_conv_norm_b, ev_w_out, od_w_in, od_ret_norm_g, od_sgu_norm_g, od_sgu_norm_b, od_w_spatial, od_b_spatial, od_w_out, ln_mix_g, ln_mix_b, ln_ffn_g, ln_ffn_b, w_router, w_gate, w_up, w_down):
    n_even = ev_w_in.shape[0]
    lr0 = 2 * GLA_QK + 2 * GLA_W
    cv0 = lr0 + 2 * GLA_RANK
    row = lambda a: a[:, None, :]
    wz = jnp.zeros((n_even, 2, LANES, GLA_QK), F32)
    for z in range(2):
        wz = wz.at[:, z, z * GLA_RANK:(z + 1) * GLA_RANK, :].set(ev_w_gate_up[:, z])
    p = {
        "ev_w_main": jnp.concatenate([ev_w_in[:, :, :lr0], ev_w_in[:, :, cv0:]], axis=-1).astype(BF16),
        "ev_w_lr": jnp.pad(ev_w_in[:, :, lr0:cv0], ((0, 0), (0, 0), (0, LANES - 2 * GLA_RANK))).astype(BF16),
        "ev_wz": wz.astype(BF16),
        "ev_bz": ev_b_gate[:, :, None, :],
        "ev_gla_norm_g": row(ev_gla_norm_g),
        "ev_conv_w": jnp.pad(ev_conv_w, ((0, 0), (0, 1), (0, 0))),
        "ev_conv_b": row(ev_conv_b),
        "ev_conv_norm_g": row(ev_conv_norm_g),
        "ev_conv_norm_b": row(ev_conv_norm_b),
        "ev_w_out": ev_w_out.astype(BF16),
        "od_w_in": od_w_in.astype(BF16),
        "od_ret_norm_g": row(od_ret_norm_g),
        "od_sgu_norm_g": row(od_sgu_norm_g),
        "od_sgu_norm_b": row(od_sgu_norm_b),
        "od_w_spatial": od_w_spatial.astype(BF16),
        "od_b_spatial": od_b_spatial[:, :, :, None],
        "od_w_out": od_w_out.astype(BF16),
        "ln_mix_g": row(ln_mix_g), "ln_mix_b": row(ln_mix_b),
        "ln_ffn_g": row(ln_ffn_g), "ln_ffn_b": row(ln_ffn_b),
        "w_router_t": jnp.swapaxes(w_router, 1, 2),
        "w_gate": w_gate.astype(BF16), "w_up": w_up.astype(BF16), "w_down": w_down.astype(BF16),
    }
    return _trunk(x_prompt, p), _trunk(x_sample, p)
```

```python
import functools

import jax
import jax.numpy as jnp
from jax import lax
from jax.experimental import pallas as pl
from jax.experimental.pallas import tpu as pltpu

F32 = jnp.float32
BF16 = jnp.bfloat16
I32 = jnp.int32

D_MODEL = 2048
DEPTH = 4
GLA_HEADS = 4
GLA_DV = 256
GLA_DK = 128
GLA_QK = GLA_HEADS * GLA_DK
GLA_W = GLA_HEADS * GLA_DV
GLA_RANK = 16
GLA_TAU = 16.0
GLA_CHUNK = 64
GLA_SUB = 16
CONV_CH = 1024
CONV_WIDTH = 31
RET_HEADS = 4
RET_DK = 256
RET_DV = 256
RET_CHUNK = 128
SGU_CH = 1024
SGU_GROUPS = 4
SGU_CHUNK = 128
N_EXPERTS = 16
CAP_FACTOR = 2
D_EXPERT = 2816
ALPHA = (2 * DEPTH) ** 0.25
LN_EPS = 1e-5

LANES = 128
SUBLANES = 8
VMEM_LIMIT = 56 * 1024 * 1024

NT_DIMS = (((1,), (1,)), ((), ()))
TN_DIMS = (((0,), (0,)), ((), ()))


def _params(*sem):
    return pltpu.CompilerParams(dimension_semantics=sem, vmem_limit_bytes=VMEM_LIMIT)


def _dot(a, b):
    return jnp.dot(a, b, preferred_element_type=F32)


def _dot_f32(a, b):
    return jnp.dot(a, b, precision=lax.Precision.HIGHEST, preferred_element_type=F32)


def _sigmoid(x):
    return 1.0 / (1.0 + jnp.exp(-x))


def _ln_rows(v, g, b):
    mu = jnp.mean(v, axis=-1, keepdims=True)
    d = v - mu
    var = jnp.mean(d * d, axis=-1, keepdims=True)
    return d * lax.rsqrt(var + LN_EPS) * g + b


def _head_norm_gate(o, g, ng):
    mu = jnp.mean(o, axis=-1, keepdims=True)
    d = o - mu
    var = jnp.mean(d * d, axis=-1, keepdims=True)
    return d * lax.rsqrt(var + LN_EPS) * ng * (g * _sigmoid(g))


def _mm_kernel(x_ref, w_ref, o_ref):
    o_ref[...] = _dot(x_ref[...].astype(BF16), w_ref[...])


def _matmul(x, w, bm, bn, name):
    m, k = x.shape
    n = w.shape[1]
    return pl.pallas_call(
        _mm_kernel,
        grid=(m // bm, n // bn),
        in_specs=[pl.BlockSpec((bm, k), lambda i, j: (i, 0)),
                  pl.BlockSpec((k, bn), lambda i, j: (0, j))],
        out_specs=pl.BlockSpec((bm, bn), lambda i, j: (i, j)),
        out_shape=jax.ShapeDtypeStruct((m, n), F32),
        compiler_params=_params("parallel", "arbitrary"),
        name=name,
    )(x, w)


GLA_TB = 256


def _gla_kernel(rev, *refs):
    if rev:
        (q_ref, k_ref, v_ref, lr_ref, wz_ref, bz_ref, tri_ref, of_ref, g_ref, ng_ref,
         out_ref, st_ref, cum_s, q_s, k_s, v_s, o_s, qe_s, kd_s, vb_s) = refs
    else:
        (q_ref, k_ref, v_ref, lr_ref, wz_ref, bz_ref, tri_ref,
         out_ref, st_ref, cum_s, q_s, k_s, v_s, o_s, qe_s, kd_s, vb_s) = refs
    c, s, tb = GLA_CHUNK, GLA_SUB, GLA_TB
    nsub = c // s
    nch = tb // c
    heads = [(slice(h * GLA_DK, (h + 1) * GLA_DK), slice(h * GLA_DV, (h + 1) * GLA_DV))
             for h in range(GLA_HEADS)]

    @pl.when(pl.program_id(1) == 0)
    def _():
        st_ref[...] = jnp.zeros_like(st_ref)

    q = q_ref[...] * (GLA_DK ** -0.5)
    k = k_ref[...]
    v = v_ref[...]
    logits = _dot(lr_ref[...].astype(BF16), wz_ref[...]) + bz_ref[...]
    la = (jnp.minimum(logits, 0.0) - jnp.log1p(jnp.exp(-jnp.abs(logits)))) * (1.0 / GLA_TAU)
    cum = _dot_f32(tri_ref[...], la)
    end_row = [cc * c + (0 if rev else c - 1) for cc in range(nch)]
    tot_rows = jnp.concatenate(
        [jnp.broadcast_to(cum[r:r + 1], (c, GLA_QK)) for r in end_row], axis=0)
    cum_s[...] = cum
    q_s[...] = q
    k_s[...] = k
    v_s[...] = v
    qe_s[...] = (q * jnp.exp(cum)).astype(BF16)
    kd_s[...] = (k * jnp.exp(tot_rows - cum)).astype(BF16)
    vb_s[...] = v.astype(BF16)

    for cc in range(nch):
        for blk in range(nsub):
            r0 = cc * c + blk * s
            rows = slice(r0, r0 + s)
            if rev:
                bidx, js = r0 + s - 1, slice(r0 + s, (cc + 1) * c)
                empty = blk == nsub - 1
            else:
                bidx, js = r0, slice(cc * c, r0)
                empty = blk == 0
            if empty:
                o_s[rows, :] = jnp.zeros((s, GLA_W), F32)
                continue
            b = cum_s[bidx:bidx + 1, :]
            qt = (q_s[rows, :] * jnp.exp(cum_s[rows, :] - b)).astype(BF16)
            kt = (k_s[js, :] * jnp.exp(b - cum_s[js, :])).astype(BF16)
            for hk, hv in heads:
                att = lax.dot_general(qt[:, hk], kt[:, hk], NT_DIMS, preferred_element_type=F32)
                o_s[rows, hv] = _dot(att.astype(BF16), vb_s[js, hv])

    ridx = lax.broadcasted_iota(I32, (s, 1), 0)

    def sub_block(sb, carry):
        base = pl.multiple_of(sb * s, s)
        for hk, hv in heads:
            ci_ = cum_s[pl.ds(base, s), hk]
            qi = q_s[pl.ds(base, s), hk]
            ki = k_s[pl.ds(base, s), hk]
            vi = v_s[pl.ds(base, s), hv]
            acc = o_s[pl.ds(base, s), hv]
            for j in range(s):
                m = (ridx <= j) if rev else (ridx > j)
                p = qi * ki[j:j + 1] * jnp.exp(jnp.where(m, ci_ - ci_[j:j + 1], -jnp.inf))
                acc = acc + jnp.sum(p, axis=-1, keepdims=True) * vi[j:j + 1]
            o_s[pl.ds(base, s), hv] = acc
        return carry

    lax.fori_loop(0, tb // s, sub_block, 0)

    for cc in (reversed(range(nch)) if rev else range(nch)):
        rows = slice(cc * c, (cc + 1) * c)
        tot = cum_s[end_row[cc]:end_row[cc] + 1, :]
        for h, (hk, hv) in enumerate(heads):
            st = st_ref[h]
            o_s[rows, hv] += lax.dot_general(qe_s[rows, hk], st.astype(BF16), NT_DIMS,
                                             preferred_element_type=F32)
            upd = lax.dot_general(vb_s[rows, hv], kd_s[rows, hk], TN_DIMS, preferred_element_type=F32)
            st_ref[h] = st * jnp.exp(tot[:, hk]) + upd

    if rev:
        for hk, hv in heads:
            y = _head_norm_gate(of_ref[:, hv] + o_s[:, hv], g_ref[:, hv], ng_ref[:, hv])
            out_ref[:, hv] = y.astype(out_ref.dtype)
    else:
        out_ref[...] = o_s[...]


def _gla(h, lr, wz, bz, tri, nb, seqlen, rev, of=None, norm_g=None):
    n = nb * seqlen
    nblk = seqlen // GLA_TB

    def blk(b, c):
        return b * nblk + ((nblk - 1 - c) if rev else c)

    in_specs = [
        pl.BlockSpec((GLA_TB, GLA_QK), lambda b, c: (blk(b, c), 0)),
        pl.BlockSpec((GLA_TB, GLA_QK), lambda b, c: (blk(b, c), 1)),
        pl.BlockSpec((GLA_TB, GLA_W), lambda b, c: (blk(b, c), 1)),
        pl.BlockSpec((GLA_TB, LANES), lambda b, c: (blk(b, c), 0)),
        pl.BlockSpec((LANES, GLA_QK), lambda b, c: (0, 0)),
        pl.BlockSpec((1, GLA_QK), lambda b, c: (0, 0)),
        pl.BlockSpec((GLA_TB, GLA_TB), lambda b, c: (0, 0)),
    ]
    args = [h, h, h, lr, wz, bz, tri]
    if rev:
        in_specs += [
            pl.BlockSpec((GLA_TB, GLA_W), lambda b, c: (blk(b, c), 0)),
            pl.BlockSpec((GLA_TB, GLA_W), lambda b, c: (blk(b, c), 2)),
            pl.BlockSpec((1, GLA_W), lambda b, c: (0, 0)),
        ]
        args += [of, h, norm_g]
    return pl.pallas_call(
        functools.partial(_gla_kernel, rev),
        grid=(nb, nblk),
        in_specs=in_specs,
        out_specs=pl.BlockSpec((GLA_TB, GLA_W), lambda b, c: (blk(b, c), 0)),
        out_shape=jax.ShapeDtypeStruct((n, GLA_W), BF16 if rev else F32),
        scratch_shapes=[
            pltpu.VMEM((GLA_HEADS, GLA_DV, GLA_DK), F32),
            pltpu.VMEM((GLA_TB, GLA_QK), F32),
            pltpu.VMEM((GLA_TB, GLA_QK), F32),
            pltpu.VMEM((GLA_TB, GLA_QK), F32),
            pltpu.VMEM((GLA_TB, GLA_W), F32),
            pltpu.VMEM((GLA_TB, GLA_W), F32),
            pltpu.VMEM((GLA_TB, GLA_QK), BF16),
            pltpu.VMEM((GLA_TB, GLA_QK), BF16),
            pltpu.VMEM((GLA_TB, GLA_W), BF16),
        ],
        compiler_params=_params("parallel", "arbitrary"),
        name="gla_bwd" if rev else "gla_fwd",
    )(*args)


def _gla_tri(rev):
    i = jnp.arange(GLA_TB)
    same = (i[:, None] // GLA_CHUNK) == (i[None, :] // GLA_CHUNK)
    order = (i[None, :] >= i[:, None]) if rev else (i[None, :] <= i[:, None])
    return (same & order).astype(F32)


CONV_TB = 512
CONV_HALO = 16
CONV_RT = 16


def _conv_kernel(a_ref, g_ref, ap_ref, gp_ref, an_ref, gn_ref, cw_ref, cb_ref, lg_ref, lb_ref,
                 out_ref, ext_s):
    t, hl = CONV_TB, CONV_HALO
    c = pl.program_id(1)
    last = pl.num_programs(1) - 1
    ext_s[hl:hl + t, :] = a_ref[...] * _sigmoid(g_ref[...])
    prev = ap_ref[...] * _sigmoid(gp_ref[...])
    nxt = an_ref[...] * _sigmoid(gn_ref[...])
    ext_s[0:hl, :] = jnp.where(c > 0, prev, 0.0)
    ext_s[hl + t:hl + t + hl, :] = jnp.where(c < last, nxt, 0.0)
    off = hl - CONV_WIDTH // 2
    win_rows = CONV_RT + 2 * hl

    def tile(i, carry):
        r0 = pl.multiple_of(i * CONV_RT, CONV_RT)
        win = ext_s[pl.ds(r0, win_rows), :]
        acc = jnp.zeros((CONV_RT, CONV_CH), F32)
        for ph in range(SUBLANES):
            rolled = win if ph == 0 else pltpu.roll(win, win_rows - ph, axis=0)
            for w in range(CONV_WIDTH):
                if (off + w) % SUBLANES == ph:
                    a0 = off + w - ph
                    acc = acc + rolled[a0:a0 + CONV_RT, :] * cw_ref[w:w + 1, :]
        y = _ln_rows(acc + cb_ref[...], lg_ref[...], lb_ref[...])
        out_ref[pl.ds(r0, CONV_RT), :] = (y * _sigmoid(y)).astype(out_ref.dtype)
        return carry

    lax.fori_loop(0, t // CONV_RT, tile, 0)


def _conv(h, cw, cb, lg, lb, nb, seqlen, col0):
    n = nb * seqlen
    nblk = seqlen // CONV_TB
    per = CONV_TB // CONV_HALO
    nhalo = n // CONV_HALO

    def cur(b, c):
        return b * nblk + c

    def prv(b, c):
        return jnp.maximum(cur(b, c) * per - 1, 0)

    def nxt(b, c):
        return jnp.minimum((cur(b, c) + 1) * per, nhalo - 1)

    row = lambda b, c: (0, 0)
    return pl.pallas_call(
        _conv_kernel,
        grid=(nb, nblk),
        in_specs=[
            pl.BlockSpec((CONV_TB, CONV_CH), lambda b, c: (cur(b, c), col0)),
            pl.BlockSpec((CONV_TB, CONV_CH), lambda b, c: (cur(b, c), col0 + 1)),
            pl.BlockSpec((CONV_HALO, CONV_CH), lambda b, c: (prv(b, c), col0)),
            pl.BlockSpec((CONV_HALO, CONV_CH), lambda b, c: (prv(b, c), col0 + 1)),
            pl.BlockSpec((CONV_HALO, CONV_CH), lambda b, c: (nxt(b, c), col0)),
            pl.BlockSpec((CONV_HALO, CONV_CH), lambda b, c: (nxt(b, c), col0 + 1)),
            pl.BlockSpec((CONV_WIDTH + 1, CONV_CH), row),
            pl.BlockSpec((1, CONV_CH), row),
            pl.BlockSpec((1, CONV_CH), row),
            pl.BlockSpec((1, CONV_CH), row),
        ],
        out_specs=pl.BlockSpec((CONV_TB, CONV_CH), lambda b, c: (cur(b, c), 0)),
        out_shape=jax.ShapeDtypeStruct((n, CONV_CH), BF16),
        scratch_shapes=[pltpu.VMEM((CONV_TB + 2 * CONV_HALO, CONV_CH), F32)],
        compiler_params=_params("parallel", "arbitrary"),
        name="conv_module",
    )(h, h, h, h, h, h, cw, cb, lg, lb)


def _ret_kernel(rev, *refs):
    if rev:
        (q_ref, k_ref, v_ref, cos_ref, sin_ref, dm_ref, qd_ref, kd_ref, cd_ref, of_ref, g_ref, ng_ref,
         out_ref, st_ref) = refs
    else:
        (q_ref, k_ref, v_ref, cos_ref, sin_ref, dm_ref, qd_ref, kd_ref, cd_ref,
         out_ref, st_ref) = refs
    half = RET_DK // 2

    @pl.when(pl.program_id(1) == 0)
    def _():
        st_ref[...] = jnp.zeros_like(st_ref)

    cos = cos_ref[...]
    sin = sin_ref[...]
    for h in range(RET_HEADS):
        lo = slice(h * RET_DK, h * RET_DK + half)
        hi = slice(h * RET_DK + half, (h + 1) * RET_DK)
        hv = slice(h * RET_DV, (h + 1) * RET_DV)
        q1, q2 = q_ref[:, lo], q_ref[:, hi]
        k1, k2 = k_ref[:, lo], k_ref[:, hi]
        qr = jnp.concatenate([q1 * cos - q2 * sin, q2 * cos + q1 * sin], axis=-1)
        kr = jnp.concatenate([k1 * cos - k2 * sin, k2 * cos + k1 * sin], axis=-1) * (RET_DK ** -0.5)
        qb = qr.astype(BF16)
        vb = v_ref[:, hv].astype(BF16)
        att = lax.dot_general(qb, kr.astype(BF16), NT_DIMS, preferred_element_type=F32) * dm_ref[h]
        o = _dot(att.astype(BF16), vb) + _dot(qb, st_ref[h].astype(BF16)) * qd_ref[h]
        upd = lax.dot_general((kr * kd_ref[h]).astype(BF16), vb, TN_DIMS, preferred_element_type=F32)
        st_ref[h] = st_ref[h] * cd_ref[h] + upd
        if rev:
            y = _head_norm_gate(of_ref[:, hv] + o, g_ref[:, hv], ng_ref[:, hv])
            out_ref[:, hv] = y.astype(out_ref.dtype)
        else:
            out_ref[:, hv] = o


def _ret(h, cos, sin, dm, qd, kd, cd, nb, seqlen, rev, of=None, norm_g=None):
    n = nb * seqlen
    nblk = seqlen // RET_CHUNK
    w = RET_HEADS * RET_DK

    def pos(b, c):
        return (nblk - 1 - c) if rev else c

    def blk(b, c):
        return b * nblk + pos(b, c)

    const3 = lambda b, c: (0, 0, 0)
    in_specs = [
        pl.BlockSpec((RET_CHUNK, w), lambda b, c: (blk(b, c), 0)),
        pl.BlockSpec((RET_CHUNK, w), lambda b, c: (blk(b, c), 1)),
        pl.BlockSpec((RET_CHUNK, w), lambda b, c: (blk(b, c), 2)),
        pl.BlockSpec((RET_CHUNK, RET_DK // 2), lambda b, c: (pos(b, c), 0)),
        pl.BlockSpec((RET_CHUNK, RET_DK // 2), lambda b, c: (pos(b, c), 0)),
        pl.BlockSpec((RET_HEADS, RET_CHUNK, RET_CHUNK), const3),
        pl.BlockSpec((RET_HEADS, RET_CHUNK, 1), const3),
        pl.BlockSpec((RET_HEADS, RET_CHUNK, 1), const3),
        pl.BlockSpec((RET_HEADS, 1, RET_DV), const3),
    ]
    args = [h, h, h, cos, sin, dm, qd, kd, cd]
    if rev:
        in_specs += [
            pl.BlockSpec((RET_CHUNK, w), lambda b, c: (blk(b, c), 0)),
            pl.BlockSpec((RET_CHUNK, w), lambda b, c: (blk(b, c), 3)),
            pl.BlockSpec((1, w), lambda b, c: (0, 0)),
        ]
        args += [of, h, norm_g]
    return pl.pallas_call(
        functools.partial(_ret_kernel, rev),
        grid=(nb, nblk),
        in_specs=in_specs,
        out_specs=pl.BlockSpec((RET_CHUNK, w), lambda b, c: (blk(b, c), 0)),
        out_shape=jax.ShapeDtypeStruct((n, w), BF16 if rev else F32),
        scratch_shapes=[pltpu.VMEM((RET_HEADS, RET_DK, RET_DV), F32)],
        compiler_params=_params("parallel", "arbitrary"),
        name="ret_bwd" if rev else "ret_fwd",
    )(*args)


def _ret_consts(log_g, rev):
    c = RET_CHUNK
    idx = jnp.arange(c, dtype=F32)
    lg = log_g[:, None]
    if rev:
        diff = idx[None, :] - idx[:, None]
        keep = diff >= 0
        qd = jnp.exp(lg * (c - idx))
        kd = jnp.exp(lg * idx)
    else:
        diff = idx[:, None] - idx[None, :]
        keep = diff > 0
        qd = jnp.exp(lg * (idx + 1.0))
        kd = jnp.exp(lg * (c - 1.0 - idx))
    dm = jnp.where(keep[None], jnp.exp(log_g[:, None, None] * jnp.maximum(diff, 0.0)[None]), 0.0)
    cd = jnp.broadcast_to(jnp.exp(log_g * c)[:, None, None], (RET_HEADS, 1, RET_DV))
    return dm, qd[:, :, None], kd[:, :, None], cd


SGU_TB = 512


def _gelu(x):
    return 0.5 * x * (1.0 + lax.erf(x * (2.0 ** -0.5)))


def _sgu_kernel(u_ref, z_ref, lg_ref, lb_ref, ws_ref, bs_ref, out_ref):
    gw = SGU_CH // SGU_GROUPS
    for ci in range(SGU_TB // SGU_CHUNK):
        rows = slice(ci * SGU_CHUNK, (ci + 1) * SGU_CHUNK)
        u = _gelu(u_ref[rows, :])
        zn = _ln_rows(_gelu(z_ref[rows, :]), lg_ref[...], lb_ref[...]).astype(BF16)
        for g in range(SGU_GROUPS):
            cols = slice(g * gw, (g + 1) * gw)
            sp = _dot(ws_ref[g], zn[:, cols]) + bs_ref[g]
            out_ref[rows, cols] = (u[:, cols] * sp).astype(out_ref.dtype)


def _sgu(h, lg, lb, ws, bs, n, col0):
    const3 = lambda i: (0, 0, 0)
    return pl.pallas_call(
        _sgu_kernel,
        grid=(n // SGU_TB,),
        in_specs=[
            pl.BlockSpec((SGU_TB, SGU_CH), lambda i: (i, col0)),
            pl.BlockSpec((SGU_TB, SGU_CH), lambda i: (i, col0 + 1)),
            pl.BlockSpec((1, SGU_CH), lambda i: (0, 0)),
            pl.BlockSpec((1, SGU_CH), lambda i: (0, 0)),
            pl.BlockSpec((SGU_GROUPS, SGU_CHUNK, SGU_CHUNK), const3),
            pl.BlockSpec((SGU_GROUPS, SGU_CHUNK, 1), const3),
        ],
        out_specs=pl.BlockSpec((SGU_TB, SGU_CH), lambda i: (i, 0)),
        out_shape=jax.ShapeDtypeStruct((n, SGU_CH), BF16),
        compiler_params=_params("parallel"),
        name="sgu",
    )(h, h, lg, lb, ws, bs)


OUT_BM = 512
OUT_RT = 32


def _outproj_kernel(a_ref, b_ref, w_ref, x_ref, g_ref, bb_ref, wr_ref, x1_ref, p_ref):
    half = D_MODEL // 2
    x1_ref[...] = _dot(a_ref[...], w_ref[0:half, :]) + _dot(b_ref[...], w_ref[half:D_MODEL, :])

    def norm(i, carry):
        rows = pl.ds(pl.multiple_of(i * OUT_RT, OUT_RT), OUT_RT)
        x1_ref[rows, :] = _ln_rows(ALPHA * x_ref[rows, :] + x1_ref[rows, :], g_ref[...], bb_ref[...])
        return carry

    lax.fori_loop(0, OUT_BM // OUT_RT, norm, 0)
    logits = lax.dot_general(wr_ref[...], x1_ref[...], NT_DIMS, precision=lax.Precision.HIGHEST,
                             preferred_element_type=F32)
    e = jnp.exp(logits - jnp.max(logits, axis=0, keepdims=True))
    p_ref[...] = e / jnp.sum(e, axis=0, keepdims=True)


def _outproj(a, b, w, x, g, bb, wr_t):
    n = x.shape[0]
    half = D_MODEL // 2
    row = lambda i: (0, 0)
    return pl.pallas_call(
        _outproj_kernel,
        grid=(n // OUT_BM,),
        in_specs=[
            pl.BlockSpec((OUT_BM, half), lambda i: (i, 0)),
            pl.BlockSpec((OUT_BM, half), lambda i: (i, 0)),
            pl.BlockSpec((D_MODEL, D_MODEL), row),
            pl.BlockSpec((OUT_BM, D_MODEL), lambda i: (i, 0)),
            pl.BlockSpec((1, D_MODEL), row),
            pl.BlockSpec((1, D_MODEL), row),
            pl.BlockSpec((N_EXPERTS, D_MODEL), row),
        ],
        out_specs=[pl.BlockSpec((OUT_BM, D_MODEL), lambda i: (i, 0)),
                   pl.BlockSpec((N_EXPERTS, OUT_BM), lambda i: (0, i))],
        out_shape=[jax.ShapeDtypeStruct((n, D_MODEL), F32),
                   jax.ShapeDtypeStruct((N_EXPERTS, n), F32)],
        compiler_params=_params("parallel"),
        name="outproj_ln_router",
    )(a, b, w, x, g, bb, wr_t)


def _prefix_mats(nblk):
    r = lax.broadcasted_iota(I32, (LANES, LANES), 0)
    cidx = lax.broadcasted_iota(I32, (LANES, LANES), 1)
    upper = (r <= cidx).astype(BF16)
    ones = jnp.ones((LANES, LANES), BF16)
    br = lax.broadcasted_iota(I32, (nblk, nblk), 0)
    bc = lax.broadcasted_iota(I32, (nblk, nblk), 1)
    before = (bc < br).astype(BF16)
    return upper, ones, before


def _sum_rest(x):
    return jnp.sum(jnp.sum(x, axis=2, keepdims=True), axis=1, keepdims=True)


def _select_kernel(cap, p_ref, sel_ref):
    nblk = p_ref.shape[1]
    bits = pltpu.bitcast(p_ref[...], I32)

    def search(i, thr):
        cand = thr | lax.shift_left(jnp.int32(1), 30 - i)
        cnt = _sum_rest((bits >= cand).astype(I32))
        return jnp.where(cnt >= cap, cand, thr)

    thr = lax.fori_loop(0, 31, search, jnp.zeros((N_EXPERTS, 1, 1), I32))
    need = (cap - _sum_rest((bits > thr).astype(I32))).astype(F32)
    upper, ones, before = _prefix_mats(nblk)
    eq = (bits == thr).astype(F32)
    eq2 = eq.reshape(N_EXPERTS * nblk, LANES).astype(BF16)
    incl = _dot(eq2, upper)
    tot = _dot(eq2, ones).astype(BF16)
    for e in range(N_EXPERTS):
        rows = slice(e * nblk, (e + 1) * nblk)
        eq_before = _dot(before, tot[rows]) + incl[rows] - eq[e]
        sel_ref[e] = jnp.where(bits[e] > thr[e], 1.0, jnp.where(eq_before < need[e], eq[e], 0.0))


def _select(probs3, cap):
    full = pl.BlockSpec(probs3.shape, lambda i: (0, 0, 0))
    return pl.pallas_call(
        functools.partial(_select_kernel, cap),
        grid=(1,),
        in_specs=[full],
        out_specs=full,
        out_shape=jax.ShapeDtypeStruct(probs3.shape, F32),
        compiler_params=_params("arbitrary"),
        name="expert_select",
    )(probs3)


COMPACT_RT = 512


def _compact_kernel(cap, sel_ref, p_ref, out_ref, col_s):
    nblk = sel_ref.shape[0]
    upper, ones, before = _prefix_mats(nblk)
    selb = sel_ref[...].astype(BF16)
    rank = _dot(selb, upper)
    tot = _dot(selb, ones)
    through = _dot(before, tot.astype(BF16)) + tot
    through_row = jnp.transpose(through)[0:1, :]
    lane = lax.broadcasted_iota(I32, (1, LANES), 1)
    w = jnp.where(lax.broadcasted_iota(I32, (nblk, LANES), 1) == 0, 1.0, tot).astype(BF16)
    blk_lane = lax.broadcasted_iota(I32, (1, nblk), 1).astype(F32)
    rank_b = rank.astype(BF16)

    def chunk(i, carry):
        r0 = pl.multiple_of(i * COMPACT_RT, COMPACT_RT)
        r = (r0 + lax.broadcasted_iota(I32, (COMPACT_RT, 1), 0)).astype(F32)
        passed = (through_row <= r).astype(BF16)
        res = _dot(passed, w)
        blk = res[:, 0:1]
        onehot = blk_lane == blk
        target = r - res[:, 1:2]
        ranks = _dot(onehot.astype(BF16), rank_b)
        off = _dot((ranks <= target).astype(BF16), ones)
        probs = _dot_f32(onehot.astype(F32), p_ref[...])
        gate = jnp.sum(jnp.where(lane.astype(F32) == off, probs, 0.0), axis=-1, keepdims=True)
        col_s[pl.ds(r0, COMPACT_RT), :] = jnp.where(lane == 0, blk * LANES + off, gate)
        return carry

    lax.fori_loop(0, cap // COMPACT_RT, chunk, 0)
    out_ref[...] = jnp.transpose(col_s[...])[0:SUBLANES, :]


def _compact(sel3, probs3, cap):
    nblk = sel3.shape[1]
    blk = pl.BlockSpec((None, nblk, LANES), lambda e: (e, 0, 0))
    return pl.pallas_call(
        functools.partial(_compact_kernel, cap),
        grid=(N_EXPERTS,),
        in_specs=[blk, blk],
        out_specs=pl.BlockSpec((None, SUBLANES, cap), lambda e: (e, 0, 0)),
        out_shape=jax.ShapeDtypeStruct((N_EXPERTS, SUBLANES, cap), F32),
        scratch_shapes=[pltpu.VMEM((cap, LANES), F32)],
        compiler_params=_params("parallel"),
        name="expert_compact",
    )(sel3, probs3)


ROWS = 256
ISSUE_UNROLL = 8
UP_BM = 512
UP_FT = D_EXPERT // 2
DOWN_BM = 512


def _row_copy(src_hbm, dst, idx_ref, sem, gather):
    def issue(r, carry):
        t = idx_ref[0, r]
        if gather:
            pltpu.make_async_copy(src_hbm.at[pl.ds(t, 1)], dst.at[pl.ds(r, 1)], sem).start()
        else:
            pltpu.make_async_copy(dst.at[pl.ds(r, 1)], src_hbm.at[pl.ds(t, 1)], sem).start()
        return carry

    lax.fori_loop(0, ROWS, issue, 0, unroll=ISSUE_UNROLL)


def _wait_rows(hbm, buf, sem, gather):
    if gather:
        pltpu.make_async_copy(hbm.at[pl.ds(0, ROWS)], buf, sem).wait()
    else:
        pltpu.make_async_copy(buf, hbm.at[pl.ds(0, ROWS)], sem).wait()


def _gather_kernel(idx_ref, nxt_ref, x_hbm, out_ref, buf, sem):
    i = pl.program_id(0)
    slot = i % 2

    @pl.when(i == 0)
    def _():
        _row_copy(x_hbm, buf.at[0], idx_ref, sem.at[0], True)

    @pl.when(i + 1 < pl.num_programs(0))
    def _():
        _row_copy(x_hbm, buf.at[1 - slot], nxt_ref, sem.at[1 - slot], True)

    _wait_rows(x_hbm, buf.at[slot], sem.at[slot], True)
    out_ref[...] = buf[slot].astype(out_ref.dtype)


def _gather_rows(idx3, x):
    nsteps = idx3.shape[0]
    return pl.pallas_call(
        _gather_kernel,
        grid=(nsteps,),
        in_specs=[pl.BlockSpec((None, 1, ROWS), lambda i: (i, 0, 0), memory_space=pltpu.SMEM),
                  pl.BlockSpec((None, 1, ROWS), lambda i: (jnp.minimum(i + 1, nsteps - 1), 0, 0),
                               memory_space=pltpu.SMEM),
                  pl.BlockSpec(memory_space=pl.ANY)],
        out_specs=pl.BlockSpec((ROWS, D_MODEL), lambda i: (i, 0)),
        out_shape=jax.ShapeDtypeStruct((nsteps * ROWS, D_MODEL), BF16),
        scratch_shapes=[pltpu.VMEM((2, ROWS, D_MODEL), F32), pltpu.SemaphoreType.DMA((2,))],
        compiler_params=_params("arbitrary"),
        name="moe_gather",
    )(idx3, idx3, x)


def _up_kernel(xe_ref, wg_ref, wu_ref, h_ref):
    xe = xe_ref[...]
    a = _dot(xe, wg_ref[...])
    h_ref[...] = (a * _sigmoid(a) * _dot(xe, wu_ref[...])).astype(h_ref.dtype)


def _expert_up(xe, wg, wu, cap):
    nbm = cap // UP_BM
    return pl.pallas_call(
        _up_kernel,
        grid=(N_EXPERTS, D_EXPERT // UP_FT, nbm),
        in_specs=[pl.BlockSpec((UP_BM, D_MODEL), lambda e, f, i: (e * nbm + i, 0)),
                  pl.BlockSpec((None, D_MODEL, UP_FT), lambda e, f, i: (e, 0, f)),
                  pl.BlockSpec((None, D_MODEL, UP_FT), lambda e, f, i: (e, 0, f))],
        out_specs=pl.BlockSpec((UP_BM, UP_FT), lambda e, f, i: (e * nbm + i, f)),
        out_shape=jax.ShapeDtypeStruct((N_EXPERTS * cap, D_EXPERT), BF16),
        compiler_params=_params("parallel", "parallel", "arbitrary"),
        name="expert_up",
    )(xe, wg, wu)


def _down_kernel(h_ref, wd_ref, gate_ref, y_ref):
    y_ref[...] = _dot(h_ref[...], wd_ref[...]) * gate_ref[...]


def _expert_down(hid, wd, gate_col, cap):
    nbm = cap // DOWN_BM
    return pl.pallas_call(
        _down_kernel,
        grid=(N_EXPERTS, nbm),
        in_specs=[pl.BlockSpec((DOWN_BM, D_EXPERT), lambda e, i: (e * nbm + i, 0)),
                  pl.BlockSpec((None, D_EXPERT, D_MODEL), lambda e, i: (e, 0, 0)),
                  pl.BlockSpec((DOWN_BM, 1), lambda e, i: (e * nbm + i, 0))],
        out_specs=pl.BlockSpec((DOWN_BM, D_MODEL), lambda e, i: (e * nbm + i, 0)),
        out_shape=jax.ShapeDtypeStruct((N_EXPERTS * cap, D_MODEL), F32),
        compiler_params=_params("parallel", "arbitrary"),
        name="expert_down",
    )(hid, wd, gate_col)


def _scatter_kernel(steps_per_expert, idx_ref, nxt_ref, y_ref, acc_in, acc_hbm, buf, sem_in, sem_out):
    del acc_in
    i = pl.program_id(0)
    slot = i % 2
    pos = i % steps_per_expert

    @pl.when(pos == 0)
    def _():
        _row_copy(acc_hbm, buf.at[slot], idx_ref, sem_in.at[slot], True)

    _wait_rows(acc_hbm, buf.at[slot], sem_in.at[slot], True)

    @pl.when(pos + 1 < steps_per_expert)
    def _():
        _row_copy(acc_hbm, buf.at[1 - slot], nxt_ref, sem_in.at[1 - slot], True)

    buf[slot] = buf[slot] + y_ref[...]
    _row_copy(acc_hbm, buf.at[slot], idx_ref, sem_out, False)
    _wait_rows(acc_hbm, buf.at[slot], sem_out, False)


def _scatter_add(idx3, y, acc, steps_per_expert):
    nsteps = idx3.shape[0]
    return pl.pallas_call(
        functools.partial(_scatter_kernel, steps_per_expert),
        grid=(nsteps,),
        in_specs=[pl.BlockSpec((None, 1, ROWS), lambda i: (i, 0, 0), memory_space=pltpu.SMEM),
                  pl.BlockSpec((None, 1, ROWS), lambda i: (jnp.minimum(i + 1, nsteps - 1), 0, 0),
                               memory_space=pltpu.SMEM),
                  pl.BlockSpec((ROWS, D_MODEL), lambda i: (i, 0)),
                  pl.BlockSpec(memory_space=pl.ANY)],
        out_specs=pl.BlockSpec(memory_space=pl.ANY),
        out_shape=jax.ShapeDtypeStruct(acc.shape, F32),
        scratch_shapes=[pltpu.VMEM((2, ROWS, D_MODEL), F32), pltpu.SemaphoreType.DMA((2,)),
                        pltpu.SemaphoreType.DMA],
        input_output_aliases={3: 0},
        compiler_params=_params("arbitrary"),
        name="moe_scatter_add",
    )(idx3, idx3, y, acc)


FIN_BM = 512


def _final_kernel(x_ref, f_ref, g_ref, b_ref, o_ref):
    o_ref[...] = _ln_rows(ALPHA * x_ref[...] + f_ref[...], g_ref[...], b_ref[...])


def _final_ln(x, f, g, b):
    n = x.shape[0]
    tile = pl.BlockSpec((FIN_BM, D_MODEL), lambda i: (i, 0))
    row = pl.BlockSpec((1, D_MODEL), lambda i: (0, 0))
    return pl.pallas_call(
        _final_kernel,
        grid=(n // FIN_BM,),
        in_specs=[tile, tile, row, row],
        out_specs=tile,
        out_shape=jax.ShapeDtypeStruct((n, D_MODEL), F32),
        compiler_params=_params("parallel"),
        name="ffn_residual_ln",
    )(x, f, g, b)


def _route(probs_t):
    n = probs_t.shape[1]
    cap = CAP_FACTOR * n // N_EXPERTS
    probs3 = probs_t.reshape(N_EXPERTS, n // LANES, LANES)
    packed = _compact(_select(probs3, cap), probs3, cap)
    return packed[:, 0, :].astype(I32), packed[:, 1, :], cap


def _moe(x1, probs_t, wg, wu, wd):
    n = x1.shape[0]
    idx, gate, cap = _route(probs_t)
    idx3 = idx.reshape(N_EXPERTS * cap // ROWS, 1, ROWS)
    xe = _gather_rows(idx3, x1)
    hid = _expert_up(xe, wg, wu, cap)
    y = _expert_down(hid, wd, gate.reshape(N_EXPERTS * cap, 1), cap)
    return _scatter_add(idx3, y, jnp.zeros((n, D_MODEL), F32), cap // ROWS)


def _rotary_tables(seqlen):
    half = RET_DK // 2
    inv = 1.0 / (10000.0 ** jnp.linspace(0.0, 1.0, half, dtype=F32))
    ang = jnp.arange(seqlen, dtype=F32)[:, None] * inv[None, :]
    return jnp.cos(ang), jnp.sin(ang)


def _even_layer(x, p, i, nb, seqlen):
    h = _matmul(x, p["ev_w_main"][i], 1024, 1024, "in_proj_even")
    lr = _matmul(x, p["ev_w_lr"][i], 1024, LANES, "in_proj_lowrank")
    of = _gla(h, lr, p["ev_wz"][i, 0], p["ev_bz"][i, 0], _gla_tri(False), nb, seqlen, False)
    out_a = _gla(h, lr, p["ev_wz"][i, 1], p["ev_bz"][i, 1], _gla_tri(True), nb, seqlen, True,
                 of=of, norm_g=p["ev_gla_norm_g"][i])
    out_b = _conv(h, p["ev_conv_w"][i], p["ev_conv_b"][i], p["ev_conv_norm_g"][i], p["ev_conv_norm_b"][i],
                  nb, seqlen, 3)
    return out_a, out_b, p["ev_w_out"][i]


def _odd_layer(x, p, i, nb, seqlen):
    n = x.shape[0]
    h = _matmul(x, p["od_w_in"][i], 1024, 1024, "in_proj_odd")
    cos, sin = _rotary_tables(seqlen)
    hh = jnp.arange(RET_HEADS, dtype=F32)
    cf = _ret_consts(jnp.log1p(-jnp.exp2(-5.0 - 2.0 * hh)), False)
    cb = _ret_consts(jnp.log1p(-jnp.exp2(-6.0 - 2.0 * hh)), True)
    of = _ret(h, cos, sin, *cf, nb, seqlen, False)
    out_c = _ret(h, cos, sin, *cb, nb, seqlen, True, of=of, norm_g=p["od_ret_norm_g"][i])
    out_d = _sgu(h, p["od_sgu_norm_g"][i], p["od_sgu_norm_b"][i], p["od_w_spatial"][i], p["od_b_spatial"][i], n, 4)
    return out_c, out_d, p["od_w_out"][i]


def _trunk(x3, p):
    nb, seqlen, _ = x3.shape
    x = x3.reshape(nb * seqlen, D_MODEL)
    for layer in range(DEPTH):
        i = layer // 2
        mix = _even_layer if layer % 2 == 0 else _odd_layer
        a, b, w_out = mix(x, p, i, nb, seqlen)
        x1, probs_t = _outproj(a, b, w_out, x, p["ln_mix_g"][layer], p["ln_mix_b"][layer], p["w_router_t"][layer])
        f = _moe(x1, probs_t, p["w_gate"][layer], p["w_up"][layer], p["w_down"][layer])
        x = _final_ln(x1, f, p["ln_ffn_g"][layer], p["ln_ffn_b"][layer])
    return x.reshape(nb, seqlen, D_MODEL)


def kernel(x_prompt, x_sample, ev_w_in, ev_w_gate_up, ev_b_gate, ev_gla_norm_g, ev_conv_w, ev_conv_b, ev_conv_norm_g, ev_conv_norm_b, ev_w_out, od_w_in, od_ret_norm_g, od_sgu_norm_g, od_sgu_norm_b, od_w_spatial, od_b_spatial, od_w_out, ln_mix_g, ln_mix_b, ln_ffn_g, ln_ffn_b, w_router, w_gate, w_up, w_down):
    n_even = ev_w_in.shape[0]
    lr0 = 2 * GLA_QK + 2 * GLA_W
    cv0 = lr0 + 2 * GLA_RANK
    row = lambda a: a[:, None, :]
    wz = jnp.zeros((n_even, 2, LANES, GLA_QK), F32)
    for z in range(2):
        wz = wz.at[:, z, z * GLA_RANK:(z + 1) * GLA_RANK, :].set(ev_w_gate_up[:, z])
    p = {
        "ev_w_main": jnp.concatenate([ev_w_in[:, :, :lr0], ev_w_in[:, :, cv0:]], axis=-1).astype(BF16),
        "ev_w_lr": jnp.pad(ev_w_in[:, :, lr0:cv0], ((0, 0), (0, 0), (0, LANES - 2 * GLA_RANK))).astype(BF16),
        "ev_wz": wz.astype(BF16),
        "ev_bz": ev_b_gate[:, :, None, :],
        "ev_gla_norm_g": row(ev_gla_norm_g),
        "ev_conv_w": jnp.pad(ev_conv_w, ((0, 0), (0, 1), (0, 0))),
        "ev_conv_b": row(ev_conv_b),
        "ev_conv_norm_g": row(ev_conv_norm_g),
        "ev_conv_norm_b": row(ev_conv_norm_b),
        "ev_w_out": ev_w_out.astype(BF16),
        "od_w_in": od_w_in.astype(BF16),
        "od_ret_norm_g": row(od_ret_norm_g),
        "od_sgu_norm_g": row(od_sgu_norm_g),
        "od_sgu_norm_b": row(od_sgu_norm_b),
        "od_w_spatial": od_w_spatial.astype(BF16),
        "od_b_spatial": od_b_spatial[:, :, :, None],
        "od_w_out": od_w_out.astype(BF16),
        "ln_mix_g": row(ln_mix_g), "ln_mix_b": row(ln_mix_b),
        "ln_ffn_g": row(ln_ffn_g), "ln_ffn_b": row(ln_ffn_b),
        "w_router_t": jnp.swapaxes(w_router, 1, 2),
        "w_gate": w_gate.astype(BF16), "w_up": w_up.astype(BF16), "w_down": w_down.astype(BF16),
    }
    return _trunk(x_prompt, p), _trunk(x_sample, p)
```

```python
import functools

import jax
import jax.numpy as jnp
from jax import lax
from jax.experimental import pallas as pl
from jax.experimental.pallas import tpu as pltpu

F32 = jnp.float32
BF16 = jnp.bfloat16
I32 = jnp.int32

D_MODEL = 2048
DEPTH = 4
GLA_HEADS = 4
GLA_DV = 256
GLA_DK = 128
GLA_QK = GLA_HEADS * GLA_DK
GLA_W = GLA_HEADS * GLA_DV
GLA_RANK = 16
GLA_TAU = 16.0
GLA_CHUNK = 64
GLA_SUB = 16
CONV_CH = 1024
CONV_WIDTH = 31
RET_HEADS = 4
RET_DK = 256
RET_DV = 256
RET_CHUNK = 128
SGU_CH = 1024
SGU_GROUPS = 4
SGU_CHUNK = 128
N_EXPERTS = 16
CAP_FACTOR = 2
D_EXPERT = 2816
ALPHA = (2 * DEPTH) ** 0.25
LN_EPS = 1e-5

LANES = 128
SUBLANES = 8
VMEM_LIMIT = 56 * 1024 * 1024

NT_DIMS = (((1,), (1,)), ((), ()))
TN_DIMS = (((0,), (0,)), ((), ()))


def _params(*sem):
    return pltpu.CompilerParams(dimension_semantics=sem, vmem_limit_bytes=VMEM_LIMIT)


def _dot(a, b):
    return jnp.dot(a, b, preferred_element_type=F32)


def _dot_f32(a, b):
    return jnp.dot(a, b, precision=lax.Precision.HIGHEST, preferred_element_type=F32)


def _sigmoid(x):
    return 1.0 / (1.0 + jnp.exp(-x))


def _ln_rows(v, g, b):
    mu = jnp.mean(v, axis=-1, keepdims=True)
    d = v - mu
    var = jnp.mean(d * d, axis=-1, keepdims=True)
    return d * lax.rsqrt(var + LN_EPS) * g + b


def _head_norm_gate(o, g, ng):
    mu = jnp.mean(o, axis=-1, keepdims=True)
    d = o - mu
    var = jnp.mean(d * d, axis=-1, keepdims=True)
    return d * lax.rsqrt(var + LN_EPS) * ng * (g * _sigmoid(g))


def _mm_kernel(x_ref, w_ref, o_ref):
    o_ref[...] = _dot(x_ref[...].astype(BF16), w_ref[...])


def _matmul(x, w, bm, bn, name):
    m, k = x.shape
    n = w.shape[1]
    return pl.pallas_call(
        _mm_kernel,
        grid=(m // bm, n // bn),
        in_specs=[pl.BlockSpec((bm, k), lambda i, j: (i, 0)),
                  pl.BlockSpec((k, bn), lambda i, j: (0, j))],
        out_specs=pl.BlockSpec((bm, bn), lambda i, j: (i, j)),
        out_shape=jax.ShapeDtypeStruct((m, n), F32),
        compiler_params=_params("parallel", "arbitrary"),
        name=name,
    )(x, w)


GLA_TB = 256


def _gla_kernel(rev, *refs):
    if rev:
        (q_ref, k_ref, v_ref, lr_ref, wz_ref, bz_ref, tri_ref, of_ref, g_ref, ng_ref,
         out_ref, st_ref, cum_s, q_s, k_s, v_s, o_s, qe_s, kd_s, vb_s) = refs
    else:
        (q_ref, k_ref, v_ref, lr_ref, wz_ref, bz_ref, tri_ref,
         out_ref, st_ref, cum_s, q_s, k_s, v_s, o_s, qe_s, kd_s, vb_s) = refs
    c, s, tb = GLA_CHUNK, GLA_SUB, GLA_TB
    nsub = c // s
    nch = tb // c
    heads = [(slice(h * GLA_DK, (h + 1) * GLA_DK), slice(h * GLA_DV, (h + 1) * GLA_DV))
             for h in range(GLA_HEADS)]

    @pl.when(pl.program_id(1) == 0)
    def _():
        st_ref[...] = jnp.zeros_like(st_ref)

    q = q_ref[...] * (GLA_DK ** -0.5)
    k = k_ref[...]
    v = v_ref[...]
    logits = _dot(lr_ref[...].astype(BF16), wz_ref[...]) + bz_ref[...]
    la = (jnp.minimum(logits, 0.0) - jnp.log1p(jnp.exp(-jnp.abs(logits)))) * (1.0 / GLA_TAU)
    cum = _dot_f32(tri_ref[...], la)
    end_row = [cc * c + (0 if rev else c - 1) for cc in range(nch)]
    tot_rows = jnp.concatenate(
        [jnp.broadcast_to(cum[r:r + 1], (c, GLA_QK)) for r in end_row], axis=0)
    cum_s[...] = cum
    q_s[...] = q
    k_s[...] = k
    v_s[...] = v
    qe_s[...] = (q * jnp.exp(cum)).astype(BF16)
    kd_s[...] = (k * jnp.exp(tot_rows - cum)).astype(BF16)
    vb_s[...] = v.astype(BF16)

    for cc in range(nch):
        for blk in range(nsub):
            r0 = cc * c + blk * s
            rows = slice(r0, r0 + s)
            if rev:
                bidx, js = r0 + s - 1, slice(r0 + s, (cc + 1) * c)
                empty = blk == nsub - 1
            else:
                bidx, js = r0, slice(cc * c, r0)
                empty = blk == 0
            if empty:
                o_s[rows, :] = jnp.zeros((s, GLA_W), F32)
                continue
            b = cum_s[bidx:bidx + 1, :]
            qt = (q_s[rows, :] * jnp.exp(cum_s[rows, :] - b)).astype(BF16)
            kt = (k_s[js, :] * jnp.exp(b - cum_s[js, :])).astype(BF16)
            for hk, hv in heads:
                att = lax.dot_general(qt[:, hk], kt[:, hk], NT_DIMS, preferred_element_type=F32)
                o_s[rows, hv] = _dot(att.astype(BF16), vb_s[js, hv])

    ridx = lax.broadcasted_iota(I32, (s, 1), 0)

    def sub_block(sb, carry):
        base = pl.multiple_of(sb * s, s)
        for hk, hv in heads:
            ci_ = cum_s[pl.ds(base, s), hk]
            qi = q_s[pl.ds(base, s), hk]
            ki = k_s[pl.ds(base, s), hk]
            vi = v_s[pl.ds(base, s), hv]
            acc = o_s[pl.ds(base, s), hv]
            for j in range(s):
                m = (ridx <= j) if rev else (ridx > j)
                p = qi * ki[j:j + 1] * jnp.exp(jnp.where(m, ci_ - ci_[j:j + 1], -jnp.inf))
                acc = acc + jnp.sum(p, axis=-1, keepdims=True) * vi[j:j + 1]
            o_s[pl.ds(base, s), hv] = acc
        return carry

    lax.fori_loop(0, tb // s, sub_block, 0)

    for cc in (reversed(range(nch)) if rev else range(nch)):
        rows = slice(cc * c, (cc + 1) * c)
        tot = cum_s[end_row[cc]:end_row[cc] + 1, :]
        for h, (hk, hv) in enumerate(heads):
            st = st_ref[h]
            o_s[rows, hv] += lax.dot_general(qe_s[rows, hk], st.astype(BF16), NT_DIMS,
                                             preferred_element_type=F32)
            upd = lax.dot_general(vb_s[rows, hv], kd_s[rows, hk], TN_DIMS, preferred_element_type=F32)
            st_ref[h] = st * jnp.exp(tot[:, hk]) + upd

    if rev:
        for hk, hv in heads:
            y = _head_norm_gate(of_ref[:, hv] + o_s[:, hv], g_ref[:, hv], ng_ref[:, hv])
            out_ref[:, hv] = y.astype(out_ref.dtype)
    else:
        out_ref[...] = o_s[...]


def _gla(h, lr, wz, bz, tri, nb, seqlen, rev, of=None, norm_g=None):
    n = nb * seqlen
    nblk = seqlen // GLA_TB

    def blk(b, c):
        return b * nblk + ((nblk - 1 - c) if rev else c)

    in_specs = [
        pl.BlockSpec((GLA_TB, GLA_QK), lambda b, c: (blk(b, c), 0)),
        pl.BlockSpec((GLA_TB, GLA_QK), lambda b, c: (blk(b, c), 1)),
        pl.BlockSpec((GLA_TB, GLA_W), lambda b, c: (blk(b, c), 1)),
        pl.BlockSpec((GLA_TB, LANES), lambda b, c: (blk(b, c), 0)),
        pl.BlockSpec((LANES, GLA_QK), lambda b, c: (0, 0)),
        pl.BlockSpec((1, GLA_QK), lambda b, c: (0, 0)),
        pl.BlockSpec((GLA_TB, GLA_TB), lambda b, c: (0, 0)),
    ]
    args = [h, h, h, lr, wz, bz, tri]
    if rev:
        in_specs += [
            pl.BlockSpec((GLA_TB, GLA_W), lambda b, c: (blk(b, c), 0)),
            pl.BlockSpec((GLA_TB, GLA_W), lambda b, c: (blk(b, c), 2)),
            pl.BlockSpec((1, GLA_W), lambda b, c: (0, 0)),
        ]
        args += [of, h, norm_g]
    return pl.pallas_call(
        functools.partial(_gla_kernel, rev),
        grid=(nb, nblk),
        in_specs=in_specs,
        out_specs=pl.BlockSpec((GLA_TB, GLA_W), lambda b, c: (blk(b, c), 0)),
        out_shape=jax.ShapeDtypeStruct((n, GLA_W), BF16 if rev else F32),
        scratch_shapes=[
            pltpu.VMEM((GLA_HEADS, GLA_DV, GLA_DK), F32),
            pltpu.VMEM((GLA_TB, GLA_QK), F32),
            pltpu.VMEM((GLA_TB, GLA_QK), F32),
            pltpu.VMEM((GLA_TB, GLA_QK), F32),
            pltpu.VMEM((GLA_TB, GLA_W), F32),
            pltpu.VMEM((GLA_TB, GLA_W), F32),
            pltpu.VMEM((GLA_TB, GLA_QK), BF16),
            pltpu.VMEM((GLA_TB, GLA_QK), BF16),
            pltpu.VMEM((GLA_TB, GLA_W), BF16),
        ],
        compiler_params=_params("parallel", "arbitrary"),
        name="gla_bwd" if rev else "gla_fwd",
    )(*args)


def _gla_tri(rev):
    i = jnp.arange(GLA_TB)
    same = (i[:, None] // GLA_CHUNK) == (i[None, :] // GLA_CHUNK)
    order = (i[None, :] >= i[:, None]) if rev else (i[None, :] <= i[:, None])
    return (same & order).astype(F32)


CONV_TB = 512
CONV_HALO = 16
CONV_RT = 16


CONV_PT = 32


def _conv_kernel(a_ref, g_ref, ap_ref, gp_ref, an_ref, gn_ref, cw_ref, cb_ref, lg_ref, lb_ref,
                 out_ref, ext_s, ph_s):
    t, hl = CONV_TB, CONV_HALO
    ext_rows = t + 2 * hl
    c = pl.program_id(1)
    last = pl.num_programs(1) - 1
    ext_s[hl:hl + t, :] = a_ref[...] * _sigmoid(g_ref[...])
    prev = ap_ref[...] * _sigmoid(gp_ref[...])
    nxt = an_ref[...] * _sigmoid(gn_ref[...])
    ext_s[0:hl, :] = jnp.where(c > 0, prev, 0.0)
    ext_s[hl + t:ext_rows, :] = jnp.where(c < last, nxt, 0.0)
    ext_s[ext_rows:ext_rows + SUBLANES, :] = jnp.zeros((SUBLANES, CONV_CH), F32)
    off = hl - CONV_WIDTH // 2

    def shift(i, carry):
        r0 = pl.multiple_of(i * CONV_PT, CONV_PT)
        blk = ext_s[pl.ds(r0, CONV_PT + SUBLANES), :]
        for ph in range(1, SUBLANES):
            ph_s[ph - 1, pl.ds(r0, CONV_PT), :] = pltpu.roll(blk, CONV_PT + SUBLANES - ph, axis=0)[0:CONV_PT]
        return carry

    lax.fori_loop(0, ext_rows // CONV_PT, shift, 0)

    def tile(i, carry):
        r0 = pl.multiple_of(i * CONV_RT, CONV_RT)
        acc = jnp.zeros((CONV_RT, CONV_CH), F32)
        for w in range(CONV_WIDTH):
            ph = (off + w) % SUBLANES
            rows = pl.ds(r0 + off + w - ph, CONV_RT)
            tap = ext_s[rows, :] if ph == 0 else ph_s[ph - 1, rows, :]
            acc = acc + tap * cw_ref[w:w + 1, :]
        y = _ln_rows(acc + cb_ref[...], lg_ref[...], lb_ref[...])
        out_ref[pl.ds(r0, CONV_RT), :] = (y * _sigmoid(y)).astype(out_ref.dtype)
        return carry

    lax.fori_loop(0, t // CONV_RT, tile, 0)


def _conv(h, cw, cb, lg, lb, nb, seqlen, col0):
    n = nb * seqlen
    nblk = seqlen // CONV_TB
    per = CONV_TB // CONV_HALO
    nhalo = n // CONV_HALO

    def cur(b, c):
        return b * nblk + c

    def prv(b, c):
        return jnp.maximum(cur(b, c) * per - 1, 0)

    def nxt(b, c):
        return jnp.minimum((cur(b, c) + 1) * per, nhalo - 1)

    row = lambda b, c: (0, 0)
    return pl.pallas_call(
        _conv_kernel,
        grid=(nb, nblk),
        in_specs=[
            pl.BlockSpec((CONV_TB, CONV_CH), lambda b, c: (cur(b, c), col0)),
            pl.BlockSpec((CONV_TB, CONV_CH), lambda b, c: (cur(b, c), col0 + 1)),
            pl.BlockSpec((CONV_HALO, CONV_CH), lambda b, c: (prv(b, c), col0)),
            pl.BlockSpec((CONV_HALO, CONV_CH), lambda b, c: (prv(b, c), col0 + 1)),
            pl.BlockSpec((CONV_HALO, CONV_CH), lambda b, c: (nxt(b, c), col0)),
            pl.BlockSpec((CONV_HALO, CONV_CH), lambda b, c: (nxt(b, c), col0 + 1)),
            pl.BlockSpec((CONV_WIDTH + 1, CONV_CH), row),
            pl.BlockSpec((1, CONV_CH), row),
            pl.BlockSpec((1, CONV_CH), row),
            pl.BlockSpec((1, CONV_CH), row),
        ],
        out_specs=pl.BlockSpec((CONV_TB, CONV_CH), lambda b, c: (cur(b, c), 0)),
        out_shape=jax.ShapeDtypeStruct((n, CONV_CH), BF16),
        scratch_shapes=[pltpu.VMEM((CONV_TB + 2 * CONV_HALO + SUBLANES, CONV_CH), F32),
                        pltpu.VMEM((SUBLANES - 1, CONV_TB + 2 * CONV_HALO, CONV_CH), F32)],
        compiler_params=_params("parallel", "arbitrary"),
        name="conv_module",
    )(h, h, h, h, h, h, cw, cb, lg, lb)


def _ret_kernel(rev, *refs):
    if rev:
        (q_ref, k_ref, v_ref, cos_ref, sin_ref, dm_ref, qd_ref, kd_ref, cd_ref, of_ref, g_ref, ng_ref,
         out_ref, st_ref) = refs
    else:
        (q_ref, k_ref, v_ref, cos_ref, sin_ref, dm_ref, qd_ref, kd_ref, cd_ref,
         out_ref, st_ref) = refs
    half = RET_DK // 2

    @pl.when(pl.program_id(1) == 0)
    def _():
        st_ref[...] = jnp.zeros_like(st_ref)

    cos = cos_ref[...]
    sin = sin_ref[...]
    for h in range(RET_HEADS):
        lo = slice(h * RET_DK, h * RET_DK + half)
        hi = slice(h * RET_DK + half, (h + 1) * RET_DK)
        hv = slice(h * RET_DV, (h + 1) * RET_DV)
        q1, q2 = q_ref[:, lo], q_ref[:, hi]
        k1, k2 = k_ref[:, lo], k_ref[:, hi]
        qr = jnp.concatenate([q1 * cos - q2 * sin, q2 * cos + q1 * sin], axis=-1)
        kr = jnp.concatenate([k1 * cos - k2 * sin, k2 * cos + k1 * sin], axis=-1) * (RET_DK ** -0.5)
        qb = qr.astype(BF16)
        vb = v_ref[:, hv].astype(BF16)
        att = lax.dot_general(qb, kr.astype(BF16), NT_DIMS, preferred_element_type=F32) * dm_ref[h]
        o = _dot(att.astype(BF16), vb) + _dot(qb, st_ref[h].astype(BF16)) * qd_ref[h]
        upd = lax.dot_general((kr * kd_ref[h]).astype(BF16), vb, TN_DIMS, preferred_element_type=F32)
        st_ref[h] = st_ref[h] * cd_ref[h] + upd
        if rev:
            y = _head_norm_gate(of_ref[:, hv] + o, g_ref[:, hv], ng_ref[:, hv])
            out_ref[:, hv] = y.astype(out_ref.dtype)
        else:
            out_ref[:, hv] = o


def _ret(h, cos, sin, dm, qd, kd, cd, nb, seqlen, rev, of=None, norm_g=None):
    n = nb * seqlen
    nblk = seqlen // RET_CHUNK
    w = RET_HEADS * RET_DK

    def pos(b, c):
        return (nblk - 1 - c) if rev else c

    def blk(b, c):
        return b * nblk + pos(b, c)

    const3 = lambda b, c: (0, 0, 0)
    in_specs = [
        pl.BlockSpec((RET_CHUNK, w), lambda b, c: (blk(b, c), 0)),
        pl.BlockSpec((RET_CHUNK, w), lambda b, c: (blk(b, c), 1)),
        pl.BlockSpec((RET_CHUNK, w), lambda b, c: (blk(b, c), 2)),
        pl.BlockSpec((RET_CHUNK, RET_DK // 2), lambda b, c: (pos(b, c), 0)),
        pl.BlockSpec((RET_CHUNK, RET_DK // 2), lambda b, c: (pos(b, c), 0)),
        pl.BlockSpec((RET_HEADS, RET_CHUNK, RET_CHUNK), const3),
        pl.BlockSpec((RET_HEADS, RET_CHUNK, 1), const3),
        pl.BlockSpec((RET_HEADS, RET_CHUNK, 1), const3),
        pl.BlockSpec((RET_HEADS, 1, RET_DV), const3),
    ]
    args = [h, h, h, cos, sin, dm, qd, kd, cd]
    if rev:
        in_specs += [
            pl.BlockSpec((RET_CHUNK, w), lambda b, c: (blk(b, c), 0)),
            pl.BlockSpec((RET_CHUNK, w), lambda b, c: (blk(b, c), 3)),
            pl.BlockSpec((1, w), lambda b, c: (0, 0)),
        ]
        args += [of, h, norm_g]
    return pl.pallas_call(
        functools.partial(_ret_kernel, rev),
        grid=(nb, nblk),
        in_specs=in_specs,
        out_specs=pl.BlockSpec((RET_CHUNK, w), lambda b, c: (blk(b, c), 0)),
        out_shape=jax.ShapeDtypeStruct((n, w), BF16 if rev else F32),
        scratch_shapes=[pltpu.VMEM((RET_HEADS, RET_DK, RET_DV), F32)],
        compiler_params=_params("parallel", "arbitrary"),
        name="ret_bwd" if rev else "ret_fwd",
    )(*args)


def _ret_consts(log_g, rev):
    c = RET_CHUNK
    idx = jnp.arange(c, dtype=F32)
    lg = log_g[:, None]
    if rev:
        diff = idx[None, :] - idx[:, None]
        keep = diff >= 0
        qd = jnp.exp(lg * (c - idx))
        kd = jnp.exp(lg * idx)
    else:
        diff = idx[:, None] - idx[None, :]
        keep = diff > 0
        qd = jnp.exp(lg * (idx + 1.0))
        kd = jnp.exp(lg * (c - 1.0 - idx))
    dm = jnp.where(keep[None], jnp.exp(log_g[:, None, None] * jnp.maximum(diff, 0.0)[None]), 0.0)
    cd = jnp.broadcast_to(jnp.exp(log_g * c)[:, None, None], (RET_HEADS, 1, RET_DV))
    return dm, qd[:, :, None], kd[:, :, None], cd


SGU_TB = 512


def _gelu(x):
    return 0.5 * x * (1.0 + lax.erf(x * (2.0 ** -0.5)))


def _sgu_kernel(u_ref, z_ref, lg_ref, lb_ref, ws_ref, bs_ref, out_ref):
    gw = SGU_CH // SGU_GROUPS
    for ci in range(SGU_TB // SGU_CHUNK):
        rows = slice(ci * SGU_CHUNK, (ci + 1) * SGU_CHUNK)
        u = _gelu(u_ref[rows, :])
        zn = _ln_rows(_gelu(z_ref[rows, :]), lg_ref[...], lb_ref[...]).astype(BF16)
        for g in range(SGU_GROUPS):
            cols = slice(g * gw, (g + 1) * gw)
            sp = _dot(ws_ref[g], zn[:, cols]) + bs_ref[g]
            out_ref[rows, cols] = (u[:, cols] * sp).astype(out_ref.dtype)


def _sgu(h, lg, lb, ws, bs, n, col0):
    const3 = lambda i: (0, 0, 0)
    return pl.pallas_call(
        _sgu_kernel,
        grid=(n // SGU_TB,),
        in_specs=[
            pl.BlockSpec((SGU_TB, SGU_CH), lambda i: (i, col0)),
            pl.BlockSpec((SGU_TB, SGU_CH), lambda i: (i, col0 + 1)),
            pl.BlockSpec((1, SGU_CH), lambda i: (0, 0)),
            pl.BlockSpec((1, SGU_CH), lambda i: (0, 0)),
            pl.BlockSpec((SGU_GROUPS, SGU_CHUNK, SGU_CHUNK), const3),
            pl.BlockSpec((SGU_GROUPS, SGU_CHUNK, 1), const3),
        ],
        out_specs=pl.BlockSpec((SGU_TB, SGU_CH), lambda i: (i, 0)),
        out_shape=jax.ShapeDtypeStruct((n, SGU_CH), BF16),
        compiler_params=_params("parallel"),
        name="sgu",
    )(h, h, lg, lb, ws, bs)


OUT_BM = 512
OUT_RT = 32


def _outproj_kernel(a_ref, b_ref, w_ref, x_ref, g_ref, bb_ref, wr_ref, x1_ref, res_ref, p_ref):
    x1_ref[...] = _dot(jnp.concatenate([a_ref[...], b_ref[...]], axis=-1), w_ref[...])

    def norm(i, carry):
        rows = pl.ds(pl.multiple_of(i * OUT_RT, OUT_RT), OUT_RT)
        y = _ln_rows(ALPHA * x_ref[rows, :] + x1_ref[rows, :], g_ref[...], bb_ref[...])
        x1_ref[rows, :] = y
        res_ref[rows, :] = ALPHA * y
        return carry

    lax.fori_loop(0, OUT_BM // OUT_RT, norm, 0)
    logits = lax.dot_general(wr_ref[...], x1_ref[...], NT_DIMS, precision=lax.Precision.HIGHEST,
                             preferred_element_type=F32)
    e = jnp.exp(logits - jnp.max(logits, axis=0, keepdims=True))
    p_ref[...] = e / jnp.sum(e, axis=0, keepdims=True)


def _outproj(a, b, w, x, g, bb, wr_t):
    n = x.shape[0]
    half = D_MODEL // 2
    row = lambda i: (0, 0)
    return pl.pallas_call(
        _outproj_kernel,
        grid=(n // OUT_BM,),
        in_specs=[
            pl.BlockSpec((OUT_BM, half), lambda i: (i, 0)),
            pl.BlockSpec((OUT_BM, half), lambda i: (i, 0)),
            pl.BlockSpec((D_MODEL, D_MODEL), row, pipeline_mode=pl.Buffered(1)),
            pl.BlockSpec((OUT_BM, D_MODEL), lambda i: (i, 0)),
            pl.BlockSpec((1, D_MODEL), row),
            pl.BlockSpec((1, D_MODEL), row),
            pl.BlockSpec((N_EXPERTS, D_MODEL), row),
        ],
        out_specs=[pl.BlockSpec((OUT_BM, D_MODEL), lambda i: (i, 0)),
                   pl.BlockSpec((OUT_BM, D_MODEL), lambda i: (i, 0)),
                   pl.BlockSpec((N_EXPERTS, OUT_BM), lambda i: (0, i))],
        out_shape=[jax.ShapeDtypeStruct((n, D_MODEL), F32),
                   jax.ShapeDtypeStruct((n, D_MODEL), F32),
                   jax.ShapeDtypeStruct((N_EXPERTS, n), F32)],
        compiler_params=_params("parallel"),
        name="outproj_ln_router",
    )(a, b, w, x, g, bb, wr_t)


def _prefix_mats(nblk):
    r = lax.broadcasted_iota(I32, (LANES, LANES), 0)
    cidx = lax.broadcasted_iota(I32, (LANES, LANES), 1)
    upper = (r <= cidx).astype(BF16)
    ones = jnp.ones((LANES, LANES), BF16)
    br = lax.broadcasted_iota(I32, (nblk, nblk), 0)
    bc = lax.broadcasted_iota(I32, (nblk, nblk), 1)
    before = (bc < br).astype(BF16)
    return upper, ones, before


def _sum_rest(x):
    return jnp.sum(jnp.sum(x, axis=2, keepdims=True), axis=1, keepdims=True)


def _select_kernel(cap, p_ref, sel_ref):
    nblk = p_ref.shape[1]
    bits = pltpu.bitcast(p_ref[...], I32)

    def search(i, thr):
        cand = thr | lax.shift_left(jnp.int32(1), 30 - i)
        cnt = _sum_rest((bits >= cand).astype(I32))
        return jnp.where(cnt >= cap, cand, thr)

    thr = lax.fori_loop(0, 31, search, jnp.zeros((N_EXPERTS, 1, 1), I32))
    need = (cap - _sum_rest((bits > thr).astype(I32))).astype(F32)
    upper, ones, before = _prefix_mats(nblk)
    eq = (bits == thr).astype(F32)
    eq2 = eq.reshape(N_EXPERTS * nblk, LANES).astype(BF16)
    incl = _dot(eq2, upper)
    tot = _dot(eq2, ones).astype(BF16)
    for e in range(N_EXPERTS):
        rows = slice(e * nblk, (e + 1) * nblk)
        eq_before = _dot(before, tot[rows]) + incl[rows] - eq[e]
        sel_ref[e] = jnp.where(bits[e] > thr[e], 1.0, jnp.where(eq_before < need[e], eq[e], 0.0))


def _select(probs3, cap):
    full = pl.BlockSpec(probs3.shape, lambda i: (0, 0, 0))
    return pl.pallas_call(
        functools.partial(_select_kernel, cap),
        grid=(1,),
        in_specs=[full],
        out_specs=full,
        out_shape=jax.ShapeDtypeStruct(probs3.shape, F32),
        compiler_params=_params("arbitrary"),
        name="expert_select",
    )(probs3)


COMPACT_RT = 512


def _compact_kernel(cap, sel_ref, p_ref, out_ref, col_s):
    nblk = sel_ref.shape[0]
    upper, ones, before = _prefix_mats(nblk)
    selb = sel_ref[...].astype(BF16)
    rank = _dot(selb, upper)
    tot = _dot(selb, ones)
    through = _dot(before, tot.astype(BF16)) + tot
    through_row = jnp.transpose(through)[0:1, :]
    lane = lax.broadcasted_iota(I32, (1, LANES), 1)
    w = jnp.where(lax.broadcasted_iota(I32, (nblk, LANES), 1) == 0, 1.0, tot).astype(BF16)
    blk_lane = lax.broadcasted_iota(I32, (1, nblk), 1).astype(F32)
    rank_b = rank.astype(BF16)

    def chunk(i, carry):
        r0 = pl.multiple_of(i * COMPACT_RT, COMPACT_RT)
        r = (r0 + lax.broadcasted_iota(I32, (COMPACT_RT, 1), 0)).astype(F32)
        passed = (through_row <= r).astype(BF16)
        res = _dot(passed, w)
        blk = res[:, 0:1]
        onehot = blk_lane == blk
        target = r - res[:, 1:2]
        ranks = _dot(onehot.astype(BF16), rank_b)
        off = _dot((ranks <= target).astype(BF16), ones)
        probs = _dot_f32(onehot.astype(F32), p_ref[...])
        gate = jnp.sum(jnp.where(lane.astype(F32) == off, probs, 0.0), axis=-1, keepdims=True)
        col_s[pl.ds(r0, COMPACT_RT), :] = jnp.where(lane == 0, blk * LANES + off, gate)
        return carry

    lax.fori_loop(0, cap // COMPACT_RT, chunk, 0)
    out_ref[...] = jnp.transpose(col_s[...])[0:SUBLANES, :]


def _compact(sel3, probs3, cap):
    nblk = sel3.shape[1]
    blk = pl.BlockSpec((None, nblk, LANES), lambda e: (e, 0, 0))
    return pl.pallas_call(
        functools.partial(_compact_kernel, cap),
        grid=(N_EXPERTS,),
        in_specs=[blk, blk],
        out_specs=pl.BlockSpec((None, SUBLANES, cap), lambda e: (e, 0, 0)),
        out_shape=jax.ShapeDtypeStruct((N_EXPERTS, SUBLANES, cap), F32),
        scratch_shapes=[pltpu.VMEM((cap, LANES), F32)],
        compiler_params=_params("parallel"),
        name="expert_compact",
    )(sel3, probs3)


ROWS = 256
ISSUE_UNROLL = 32
UP_BM = 512
UP_FT = D_EXPERT // 2
DOWN_BM = 512


def _row_copy(src_hbm, dst, idx_ref, sem, gather):
    def issue(r, carry):
        t = idx_ref[0, r]
        if gather:
            pltpu.make_async_copy(src_hbm.at[pl.ds(t, 1)], dst.at[pl.ds(r, 1)], sem).start()
        else:
            pltpu.make_async_copy(dst.at[pl.ds(r, 1)], src_hbm.at[pl.ds(t, 1)], sem).start()
        return carry

    lax.fori_loop(0, ROWS, issue, 0, unroll=ISSUE_UNROLL)


def _wait_rows(hbm, buf, sem, gather):
    if gather:
        pltpu.make_async_copy(hbm.at[pl.ds(0, ROWS)], buf, sem).wait()
    else:
        pltpu.make_async_copy(buf, hbm.at[pl.ds(0, ROWS)], sem).wait()


def _gather_kernel(idx_ref, nxt_ref, x_hbm, out_ref, buf, sem):
    i = pl.program_id(0)
    slot = i % 2

    @pl.when(i == 0)
    def _():
        _row_copy(x_hbm, buf.at[0], idx_ref, sem.at[0], True)

    @pl.when(i + 1 < pl.num_programs(0))
    def _():
        _row_copy(x_hbm, buf.at[1 - slot], nxt_ref, sem.at[1 - slot], True)

    _wait_rows(x_hbm, buf.at[slot], sem.at[slot], True)
    out_ref[...] = buf[slot].astype(out_ref.dtype)


def _gather_rows(idx3, x):
    nsteps = idx3.shape[0]
    return pl.pallas_call(
        _gather_kernel,
        grid=(nsteps,),
        in_specs=[pl.BlockSpec((None, 1, ROWS), lambda i: (i, 0, 0), memory_space=pltpu.SMEM),
                  pl.BlockSpec((None, 1, ROWS), lambda i: (jnp.minimum(i + 1, nsteps - 1), 0, 0),
                               memory_space=pltpu.SMEM),
                  pl.BlockSpec(memory_space=pl.ANY)],
        out_specs=pl.BlockSpec((ROWS, D_MODEL), lambda i: (i, 0)),
        out_shape=jax.ShapeDtypeStruct((nsteps * ROWS, D_MODEL), BF16),
        scratch_shapes=[pltpu.VMEM((2, ROWS, D_MODEL), F32), pltpu.SemaphoreType.DMA((2,))],
        compiler_params=_params("arbitrary"),
        name="moe_gather",
    )(idx3, idx3, x)


def _up_kernel(xe_ref, wg_ref, wu_ref, h_ref):
    xe = xe_ref[...]
    a = _dot(xe, wg_ref[...])
    h_ref[...] = (a * _sigmoid(a) * _dot(xe, wu_ref[...])).astype(h_ref.dtype)


def _expert_up(xe, wg, wu, cap, layer):
    nbm = cap // UP_BM
    return pl.pallas_call(
        _up_kernel,
        grid=(N_EXPERTS, D_EXPERT // UP_FT, nbm),
        in_specs=[pl.BlockSpec((UP_BM, D_MODEL), lambda e, f, i: (e * nbm + i, 0)),
                  pl.BlockSpec((None, None, D_MODEL, UP_FT), lambda e, f, i: (layer, e, 0, f)),
                  pl.BlockSpec((None, None, D_MODEL, UP_FT), lambda e, f, i: (layer, e, 0, f))],
        out_specs=pl.BlockSpec((UP_BM, UP_FT), lambda e, f, i: (e * nbm + i, f)),
        out_shape=jax.ShapeDtypeStruct((N_EXPERTS * cap, D_EXPERT), BF16),
        compiler_params=_params("parallel", "parallel", "arbitrary"),
        name="expert_up",
    )(xe, wg, wu)


def _down_kernel(h_ref, wd_ref, gate_ref, y_ref):
    y_ref[...] = _dot(h_ref[...], wd_ref[...]) * gate_ref[...]


def _expert_down(hid, wd, gate_col, cap, layer):
    nbm = cap // DOWN_BM
    return pl.pallas_call(
        _down_kernel,
        grid=(N_EXPERTS, nbm),
        in_specs=[pl.BlockSpec((DOWN_BM, D_EXPERT), lambda e, i: (e * nbm + i, 0)),
                  pl.BlockSpec((None, None, D_EXPERT, D_MODEL), lambda e, i: (layer, e, 0, 0)),
                  pl.BlockSpec((DOWN_BM, 1), lambda e, i: (e * nbm + i, 0))],
        out_specs=pl.BlockSpec((DOWN_BM, D_MODEL), lambda e, i: (e * nbm + i, 0)),
        out_shape=jax.ShapeDtypeStruct((N_EXPERTS * cap, D_MODEL), F32),
        compiler_params=_params("parallel", "arbitrary"),
        name="expert_down",
    )(hid, wd, gate_col)


def _scatter_kernel(steps_per_expert, idx_ref, nxt_ref, y_ref, acc_in, acc_hbm, buf, sem_in, sem_out):
    del acc_in
    i = pl.program_id(0)
    slot = i % 2
    pos = i % steps_per_expert

    @pl.when(pos == 0)
    def _():
        _row_copy(acc_hbm, buf.at[slot], idx_ref, sem_in.at[slot], True)

    _wait_rows(acc_hbm, buf.at[slot], sem_in.at[slot], True)

    @pl.when(pos + 1 < steps_per_expert)
    def _():
        _row_copy(acc_hbm, buf.at[1 - slot], nxt_ref, sem_in.at[1 - slot], True)

    buf[slot] = buf[slot] + y_ref[...]
    _row_copy(acc_hbm, buf.at[slot], idx_ref, sem_out, False)
    _wait_rows(acc_hbm, buf.at[slot], sem_out, False)


def _scatter_add(idx3, y, acc, steps_per_expert):
    nsteps = idx3.shape[0]
    return pl.pallas_call(
        functools.partial(_scatter_kernel, steps_per_expert),
        grid=(nsteps,),
        in_specs=[pl.BlockSpec((None, 1, ROWS), lambda i: (i, 0, 0), memory_space=pltpu.SMEM),
                  pl.BlockSpec((None, 1, ROWS), lambda i: (jnp.minimum(i + 1, nsteps - 1), 0, 0),
                               memory_space=pltpu.SMEM),
                  pl.BlockSpec((ROWS, D_MODEL), lambda i: (i, 0)),
                  pl.BlockSpec(memory_space=pl.ANY)],
        out_specs=pl.BlockSpec(memory_space=pl.ANY),
        out_shape=jax.ShapeDtypeStruct(acc.shape, F32),
        scratch_shapes=[pltpu.VMEM((2, ROWS, D_MODEL), F32), pltpu.SemaphoreType.DMA((2,)),
                        pltpu.SemaphoreType.DMA],
        input_output_aliases={3: 0},
        compiler_params=_params("arbitrary"),
        name="moe_scatter_add",
    )(idx3, idx3, y, acc)


FIN_BM = 512


def _final_kernel(y_ref, g_ref, b_ref, o_ref):
    o_ref[...] = _ln_rows(y_ref[...], g_ref[...], b_ref[...])


def _final_ln(y, g, b):
    n = y.shape[0]
    tile = pl.BlockSpec((FIN_BM, D_MODEL), lambda i: (i, 0))
    row = pl.BlockSpec((1, D_MODEL), lambda i: (0, 0))
    return pl.pallas_call(
        _final_kernel,
        grid=(n // FIN_BM,),
        in_specs=[tile, row, row],
        out_specs=tile,
        out_shape=jax.ShapeDtypeStruct((n, D_MODEL), F32),
        compiler_params=_params("parallel"),
        name="ffn_residual_ln",
    )(y, g, b)


def _route(probs_t):
    n = probs_t.shape[1]
    cap = CAP_FACTOR * n // N_EXPERTS
    probs3 = probs_t.reshape(N_EXPERTS, n // LANES, LANES)
    packed = _compact(_select(probs3, cap), probs3, cap)
    return packed[:, 0, :].astype(I32), packed[:, 1, :], cap


def _moe(resid, x1, probs_t, wg, wu, wd, layer):
    idx, gate, cap = _route(probs_t)
    idx3 = idx.reshape(N_EXPERTS * cap // ROWS, 1, ROWS)
    xe = _gather_rows(idx3, x1)
    hid = _expert_up(xe, wg, wu, cap, layer)
    y = _expert_down(hid, wd, gate.reshape(N_EXPERTS * cap, 1), cap, layer)
    return _scatter_add(idx3, y, resid, cap // ROWS)


def _rotary_tables(seqlen):
    half = RET_DK // 2
    inv = 1.0 / (10000.0 ** jnp.linspace(0.0, 1.0, half, dtype=F32))
    ang = jnp.arange(seqlen, dtype=F32)[:, None] * inv[None, :]
    return jnp.cos(ang), jnp.sin(ang)


def _even_layer(x, p, i, nb, seqlen):
    h = _matmul(x, p["ev_w_main"][i], 1024, 1024, "in_proj_even")
    lr = _matmul(x, p["ev_w_lr"][i], 1024, LANES, "in_proj_lowrank")
    of = _gla(h, lr, p["ev_wz"][i, 0], p["ev_bz"][i, 0], _gla_tri(False), nb, seqlen, False)
    out_a = _gla(h, lr, p["ev_wz"][i, 1], p["ev_bz"][i, 1], _gla_tri(True), nb, seqlen, True,
                 of=of, norm_g=p["ev_gla_norm_g"][i])
    out_b = _conv(h, p["ev_conv_w"][i], p["ev_conv_b"][i], p["ev_conv_norm_g"][i], p["ev_conv_norm_b"][i],
                  nb, seqlen, 3)
    return out_a, out_b, p["ev_w_out"][i]


def _odd_layer(x, p, i, nb, seqlen):
    n = x.shape[0]
    h = _matmul(x, p["od_w_in"][i], 1024, 1024, "in_proj_odd")
    cos, sin = _rotary_tables(seqlen)
    hh = jnp.arange(RET_HEADS, dtype=F32)
    cf = _ret_consts(jnp.log1p(-jnp.exp2(-5.0 - 2.0 * hh)), False)
    cb = _ret_consts(jnp.log1p(-jnp.exp2(-6.0 - 2.0 * hh)), True)
    of = _ret(h, cos, sin, *cf, nb, seqlen, False)
    out_c = _ret(h, cos, sin, *cb, nb, seqlen, True, of=of, norm_g=p["od_ret_norm_g"][i])
    out_d = _sgu(h, p["od_sgu_norm_g"][i], p["od_sgu_norm_b"][i], p["od_w_spatial"][i], p["od_b_spatial"][i], n, 4)
    return out_c, out_d, p["od_w_out"][i]


def _trunk(x3, p):
    nb, seqlen, _ = x3.shape
    x = x3.reshape(nb * seqlen, D_MODEL)
    for layer in range(DEPTH):
        i = layer // 2
        mix = _even_layer if layer % 2 == 0 else _odd_layer
        a, b, w_out = mix(x, p, i, nb, seqlen)
        x1, resid, probs_t = _outproj(a, b, w_out, x, p["ln_mix_g"][layer], p["ln_mix_b"][layer],
                                      p["w_router_t"][layer])
        y = _moe(resid, x1, probs_t, p["w_gate"], p["w_up"], p["w_down"], layer)
        x = _final_ln(y, p["ln_ffn_g"][layer], p["ln_ffn_b"][layer])
    return x.reshape(nb, seqlen, D_MODEL)


def kernel(x_prompt, x_sample, ev_w_in, ev_w_gate_up, ev_b_gate, ev_gla_norm_g, ev_conv_w, ev_conv_b, ev_conv_norm_g, ev_conv_norm_b, ev_w_out, od_w_in, od_ret_norm_g, od_sgu_norm_g, od_sgu_norm_b, od_w_spatial, od_b_spatial, od_w_out, ln_mix_g, ln_mix_b, ln_ffn_g, ln_ffn_b, w_router, w_gate, w_up, w_down):
    n_even = ev_w_in.shape[0]
    lr0 = 2 * GLA_QK + 2 * GLA_W
    cv0 = lr0 + 2 * GLA_RANK
    row = lambda a: a[:, None, :]
    wz = jnp.zeros((n_even, 2, LANES, GLA_QK), F32)
    for z in range(2):
        wz = wz.at[:, z, z * GLA_RANK:(z + 1) * GLA_RANK, :].set(ev_w_gate_up[:, z])
    p = {
        "ev_w_main": jnp.concatenate([ev_w_in[:, :, :lr0], ev_w_in[:, :, cv0:]], axis=-1).astype(BF16),
        "ev_w_lr": jnp.pad(ev_w_in[:, :, lr0:cv0], ((0, 0), (0, 0), (0, LANES - 2 * GLA_RANK))).astype(BF16),
        "ev_wz": wz.astype(BF16),
        "ev_bz": ev_b_gate[:, :, None, :],
        "ev_gla_norm_g": row(ev_gla_norm_g),
        "ev_conv_w": jnp.pad(ev_conv_w, ((0, 0), (0, 1), (0, 0))),
        "ev_conv_b": row(ev_conv_b),
        "ev_conv_norm_g": row(ev_conv_norm_g),
        "ev_conv_norm_b": row(ev_conv_norm_b),
        "ev_w_out": ev_w_out.astype(BF16),
        "od_w_in": od_w_in.astype(BF16),
        "od_ret_norm_g": row(od_ret_norm_g),
        "od_sgu_norm_g": row(od_sgu_norm_g),
        "od_sgu_norm_b": row(od_sgu_norm_b),
        "od_w_spatial": od_w_spatial.astype(BF16),
        "od_b_spatial": od_b_spatial[:, :, :, None],
        "od_w_out": od_w_out.astype(BF16),
        "ln_mix_g": row(ln_mix_g), "ln_mix_b": row(ln_mix_b),
        "ln_ffn_g": row(ln_ffn_g), "ln_ffn_b": row(ln_ffn_b),
        "w_router_t": jnp.swapaxes(w_router, 1, 2),
        "w_gate": w_gate.astype(BF16), "w_up": w_up.astype(BF16), "w_down": w_down.astype(BF16),
    }
    return _trunk(x_prompt, p), _trunk(x_sample, p)
```

```python
import functools

import jax
import jax.numpy as jnp
from jax import lax
from jax.experimental import pallas as pl
from jax.experimental.pallas import tpu as pltpu

F32 = jnp.float32
BF16 = jnp.bfloat16
I32 = jnp.int32

D_MODEL = 2048
DEPTH = 4
GLA_HEADS = 4
GLA_DV = 256
GLA_DK = 128
GLA_QK = GLA_HEADS * GLA_DK
GLA_W = GLA_HEADS * GLA_DV
GLA_RANK = 16
GLA_TAU = 16.0
GLA_CHUNK = 64
GLA_SUB = 16
CONV_CH = 1024
CONV_WIDTH = 31
RET_HEADS = 4
RET_DK = 256
RET_DV = 256
RET_CHUNK = 128
SGU_CH = 1024
SGU_GROUPS = 4
SGU_CHUNK = 128
N_EXPERTS = 16
CAP_FACTOR = 2
D_EXPERT = 2816
ALPHA = (2 * DEPTH) ** 0.25
LN_EPS = 1e-5

LANES = 128
SUBLANES = 8
VMEM_LIMIT = 56 * 1024 * 1024

NT_DIMS = (((1,), (1,)), ((), ()))
TN_DIMS = (((0,), (0,)), ((), ()))


def _params(*sem):
    return pltpu.CompilerParams(dimension_semantics=sem, vmem_limit_bytes=VMEM_LIMIT)


def _dot(a, b):
    return jnp.dot(a, b, preferred_element_type=F32)


def _dot_f32(a, b):
    return jnp.dot(a, b, precision=lax.Precision.HIGHEST, preferred_element_type=F32)


def _sigmoid(x):
    return 1.0 / (1.0 + jnp.exp(-x))


def _ln_rows(v, g, b):
    mu = jnp.mean(v, axis=-1, keepdims=True)
    d = v - mu
    var = jnp.mean(d * d, axis=-1, keepdims=True)
    return d * lax.rsqrt(var + LN_EPS) * g + b


def _head_norm_gate(o, g, ng):
    mu = jnp.mean(o, axis=-1, keepdims=True)
    d = o - mu
    var = jnp.mean(d * d, axis=-1, keepdims=True)
    return d * lax.rsqrt(var + LN_EPS) * ng * (g * _sigmoid(g))


def _mm_kernel(x_ref, w_ref, o_ref):
    o_ref[...] = _dot(x_ref[...].astype(BF16), w_ref[...])


def _matmul(x, w, bm, bn, name):
    m, k = x.shape
    n = w.shape[1]
    return pl.pallas_call(
        _mm_kernel,
        grid=(m // bm, n // bn),
        in_specs=[pl.BlockSpec((bm, k), lambda i, j: (i, 0)),
                  pl.BlockSpec((k, bn), lambda i, j: (0, j))],
        out_specs=pl.BlockSpec((bm, bn), lambda i, j: (i, j)),
        out_shape=jax.ShapeDtypeStruct((m, n), F32),
        compiler_params=_params("parallel", "arbitrary"),
        name=name,
    )(x, w)


GLA_TB = 256


def _gla_kernel(rev, *refs):
    if rev:
        (q_ref, k_ref, v_ref, lr_ref, wz_ref, bz_ref, tri_ref, of_ref, g_ref, ng_ref,
         out_ref, st_ref, cum_s, q_s, k_s, v_s, o_s, qe_s, kd_s, vb_s) = refs
    else:
        (q_ref, k_ref, v_ref, lr_ref, wz_ref, bz_ref, tri_ref,
         out_ref, st_ref, cum_s, q_s, k_s, v_s, o_s, qe_s, kd_s, vb_s) = refs
    c, s, tb = GLA_CHUNK, GLA_SUB, GLA_TB
    nsub = c // s
    nch = tb // c
    heads = [(slice(h * GLA_DK, (h + 1) * GLA_DK), slice(h * GLA_DV, (h + 1) * GLA_DV))
             for h in range(GLA_HEADS)]

    @pl.when(pl.program_id(1) == 0)
    def _():
        st_ref[...] = jnp.zeros_like(st_ref)

    q = q_ref[...] * (GLA_DK ** -0.5)
    k = k_ref[...]
    v = v_ref[...]
    logits = _dot(lr_ref[...].astype(BF16), wz_ref[...]) + bz_ref[...]
    la = (jnp.minimum(logits, 0.0) - jnp.log1p(jnp.exp(-jnp.abs(logits)))) * (1.0 / GLA_TAU)
    cum = _dot_f32(tri_ref[...], la)
    end_row = [cc * c + (0 if rev else c - 1) for cc in range(nch)]
    tot_rows = jnp.concatenate(
        [jnp.broadcast_to(cum[r:r + 1], (c, GLA_QK)) for r in end_row], axis=0)
    cum_s[...] = cum
    q_s[...] = q
    k_s[...] = k
    v_s[...] = v
    qe_s[...] = (q * jnp.exp(cum)).astype(BF16)
    kd_s[...] = (k * jnp.exp(tot_rows - cum)).astype(BF16)
    vb_s[...] = v.astype(BF16)

    for cc in range(nch):
        for blk in range(nsub):
            r0 = cc * c + blk * s
            rows = slice(r0, r0 + s)
            if rev:
                bidx, js = r0 + s - 1, slice(r0 + s, (cc + 1) * c)
                empty = blk == nsub - 1
            else:
                bidx, js = r0, slice(cc * c, r0)
                empty = blk == 0
            if empty:
                o_s[rows, :] = jnp.zeros((s, GLA_W), F32)
                continue
            b = cum_s[bidx:bidx + 1, :]
            qt = (q_s[rows, :] * jnp.exp(cum_s[rows, :] - b)).astype(BF16)
            kt = (k_s[js, :] * jnp.exp(b - cum_s[js, :])).astype(BF16)
            for hk, hv in heads:
                att = lax.dot_general(qt[:, hk], kt[:, hk], NT_DIMS, preferred_element_type=F32)
                o_s[rows, hv] = _dot(att.astype(BF16), vb_s[js, hv])

    ridx = lax.broadcasted_iota(I32, (s, 1), 0)

    def sub_block(sb, carry):
        base = pl.multiple_of(sb * s, s)
        for hk, hv in heads:
            ci_ = cum_s[pl.ds(base, s), hk]
            qi = q_s[pl.ds(base, s), hk]
            ki = k_s[pl.ds(base, s), hk]
            vi = v_s[pl.ds(base, s), hv]
            acc = o_s[pl.ds(base, s), hv]
            for j in range(s):
                m = (ridx <= j) if rev else (ridx > j)
                p = qi * ki[j:j + 1] * jnp.exp(jnp.where(m, ci_ - ci_[j:j + 1], -jnp.inf))
                acc = acc + jnp.sum(p, axis=-1, keepdims=True) * vi[j:j + 1]
            o_s[pl.ds(base, s), hv] = acc
        return carry

    lax.fori_loop(0, tb // s, sub_block, 0)

    for cc in (reversed(range(nch)) if rev else range(nch)):
        rows = slice(cc * c, (cc + 1) * c)
        tot = cum_s[end_row[cc]:end_row[cc] + 1, :]
        for h, (hk, hv) in enumerate(heads):
            st = st_ref[h]
            o_s[rows, hv] += lax.dot_general(qe_s[rows, hk], st.astype(BF16), NT_DIMS,
                                             preferred_element_type=F32)
            upd = lax.dot_general(vb_s[rows, hv], kd_s[rows, hk], TN_DIMS, preferred_element_type=F32)
            st_ref[h] = st * jnp.exp(tot[:, hk]) + upd

    if rev:
        for hk, hv in heads:
            y = _head_norm_gate(of_ref[:, hv] + o_s[:, hv], g_ref[:, hv], ng_ref[:, hv])
            out_ref[:, hv] = y.astype(out_ref.dtype)
    else:
        out_ref[...] = o_s[...]


def _gla(h, lr, wz, bz, tri, nb, seqlen, rev, of=None, norm_g=None):
    n = nb * seqlen
    nblk = seqlen // GLA_TB

    def blk(b, c):
        return b * nblk + ((nblk - 1 - c) if rev else c)

    in_specs = [
        pl.BlockSpec((GLA_TB, GLA_QK), lambda b, c: (blk(b, c), 0)),
        pl.BlockSpec((GLA_TB, GLA_QK), lambda b, c: (blk(b, c), 1)),
        pl.BlockSpec((GLA_TB, GLA_W), lambda b, c: (blk(b, c), 1)),
        pl.BlockSpec((GLA_TB, LANES), lambda b, c: (blk(b, c), 0)),
        pl.BlockSpec((LANES, GLA_QK), lambda b, c: (0, 0)),
        pl.BlockSpec((1, GLA_QK), lambda b, c: (0, 0)),
        pl.BlockSpec((GLA_TB, GLA_TB), lambda b, c: (0, 0)),
    ]
    args = [h, h, h, lr, wz, bz, tri]
    if rev:
        in_specs += [
            pl.BlockSpec((GLA_TB, GLA_W), lambda b, c: (blk(b, c), 0)),
            pl.BlockSpec((GLA_TB, GLA_W), lambda b, c: (blk(b, c), 2)),
            pl.BlockSpec((1, GLA_W), lambda b, c: (0, 0)),
        ]
        args += [of, h, norm_g]
    return pl.pallas_call(
        functools.partial(_gla_kernel, rev),
        grid=(nb, nblk),
        in_specs=in_specs,
        out_specs=pl.BlockSpec((GLA_TB, GLA_W), lambda b, c: (blk(b, c), 0)),
        out_shape=jax.ShapeDtypeStruct((n, GLA_W), BF16 if rev else F32),
        scratch_shapes=[
            pltpu.VMEM((GLA_HEADS, GLA_DV, GLA_DK), F32),
            pltpu.VMEM((GLA_TB, GLA_QK), F32),
            pltpu.VMEM((GLA_TB, GLA_QK), F32),
            pltpu.VMEM((GLA_TB, GLA_QK), F32),
            pltpu.VMEM((GLA_TB, GLA_W), F32),
            pltpu.VMEM((GLA_TB, GLA_W), F32),
            pltpu.VMEM((GLA_TB, GLA_QK), BF16),
            pltpu.VMEM((GLA_TB, GLA_QK), BF16),
            pltpu.VMEM((GLA_TB, GLA_W), BF16),
        ],
        compiler_params=_params("parallel", "arbitrary"),
        name="gla_bwd" if rev else "gla_fwd",
    )(*args)


def _gla_tri(rev):
    i = jnp.arange(GLA_TB)
    same = (i[:, None] // GLA_CHUNK) == (i[None, :] // GLA_CHUNK)
    order = (i[None, :] >= i[:, None]) if rev else (i[None, :] <= i[:, None])
    return (same & order).astype(F32)


CONV_TB = 512
CONV_HALO = 16
CONV_RT = 64


CONV_PT = 32


def _conv_kernel(a_ref, g_ref, ap_ref, gp_ref, an_ref, gn_ref, cw_ref, cb_ref, lg_ref, lb_ref,
                 out_ref, ext_s, ph_s, y_s):
    t, hl = CONV_TB, CONV_HALO
    ext_rows = t + 2 * hl
    c = pl.program_id(1)
    last = pl.num_programs(1) - 1
    ext_s[hl:hl + t, :] = a_ref[...] * _sigmoid(g_ref[...])
    prev = ap_ref[...] * _sigmoid(gp_ref[...])
    nxt = an_ref[...] * _sigmoid(gn_ref[...])
    ext_s[0:hl, :] = jnp.where(c > 0, prev, 0.0)
    ext_s[hl + t:ext_rows, :] = jnp.where(c < last, nxt, 0.0)
    ext_s[ext_rows:ext_rows + SUBLANES, :] = jnp.zeros((SUBLANES, CONV_CH), F32)
    off = hl - CONV_WIDTH // 2

    def shift(i, carry):
        r0 = pl.multiple_of(i * CONV_PT, CONV_PT)
        blk = ext_s[pl.ds(r0, CONV_PT + SUBLANES), :]
        for ph in range(1, SUBLANES):
            ph_s[ph - 1, pl.ds(r0, CONV_PT), :] = pltpu.roll(blk, CONV_PT + SUBLANES - ph, axis=0)[0:CONV_PT]
        return carry

    lax.fori_loop(0, ext_rows // CONV_PT, shift, 0)

    def tile(i, carry):
        r0 = pl.multiple_of(i * CONV_RT, CONV_RT)
        for lt in range(CONV_CH // LANES):
            cols = slice(lt * LANES, (lt + 1) * LANES)
            acc = jnp.zeros((CONV_RT, LANES), F32)
            for w in range(CONV_WIDTH):
                ph = (off + w) % SUBLANES
                rows = pl.ds(r0 + off + w - ph, CONV_RT)
                tap = ext_s[rows, cols] if ph == 0 else ph_s[ph - 1, rows, cols]
                acc = acc + tap * cw_ref[w:w + 1, cols]
            y_s[pl.ds(r0, CONV_RT), cols] = acc + cb_ref[:, cols]
        for half in range(2):
            rows = pl.ds(r0 + half * (CONV_RT // 2), CONV_RT // 2)
            y = _ln_rows(y_s[rows, :], lg_ref[...], lb_ref[...])
            out_ref[rows, :] = (y * _sigmoid(y)).astype(out_ref.dtype)
        return carry

    lax.fori_loop(0, t // CONV_RT, tile, 0)


def _conv(h, cw, cb, lg, lb, nb, seqlen, col0):
    n = nb * seqlen
    nblk = seqlen // CONV_TB
    per = CONV_TB // CONV_HALO
    nhalo = n // CONV_HALO

    def cur(b, c):
        return b * nblk + c

    def prv(b, c):
        return jnp.maximum(cur(b, c) * per - 1, 0)

    def nxt(b, c):
        return jnp.minimum((cur(b, c) + 1) * per, nhalo - 1)

    row = lambda b, c: (0, 0)
    return pl.pallas_call(
        _conv_kernel,
        grid=(nb, nblk),
        in_specs=[
            pl.BlockSpec((CONV_TB, CONV_CH), lambda b, c: (cur(b, c), col0)),
            pl.BlockSpec((CONV_TB, CONV_CH), lambda b, c: (cur(b, c), col0 + 1)),
            pl.BlockSpec((CONV_HALO, CONV_CH), lambda b, c: (prv(b, c), col0)),
            pl.BlockSpec((CONV_HALO, CONV_CH), lambda b, c: (prv(b, c), col0 + 1)),
            pl.BlockSpec((CONV_HALO, CONV_CH), lambda b, c: (nxt(b, c), col0)),
            pl.BlockSpec((CONV_HALO, CONV_CH), lambda b, c: (nxt(b, c), col0 + 1)),
            pl.BlockSpec((CONV_WIDTH + 1, CONV_CH), row),
            pl.BlockSpec((1, CONV_CH), row),
            pl.BlockSpec((1, CONV_CH), row),
            pl.BlockSpec((1, CONV_CH), row),
        ],
        out_specs=pl.BlockSpec((CONV_TB, CONV_CH), lambda b, c: (cur(b, c), 0)),
        out_shape=jax.ShapeDtypeStruct((n, CONV_CH), BF16),
        scratch_shapes=[pltpu.VMEM((CONV_TB + 2 * CONV_HALO + SUBLANES, CONV_CH), F32),
                        pltpu.VMEM((SUBLANES - 1, CONV_TB + 2 * CONV_HALO, CONV_CH), F32),
                        pltpu.VMEM((CONV_TB, CONV_CH), F32)],
        compiler_params=_params("parallel", "arbitrary"),
        name="conv_module",
    )(h, h, h, h, h, h, cw, cb, lg, lb)


RET_TB = 256


def _ret_kernel(rev, *refs):
    if rev:
        (q_ref, k_ref, v_ref, cos_ref, sin_ref, dm_ref, qd_ref, kd_ref, cd_ref, of_ref, g_ref, ng_ref,
         out_ref, st_ref) = refs
    else:
        (q_ref, k_ref, v_ref, cos_ref, sin_ref, dm_ref, qd_ref, kd_ref, cd_ref,
         out_ref, st_ref) = refs
    half = RET_DK // 2

    @pl.when(pl.program_id(1) == 0)
    def _():
        st_ref[...] = jnp.zeros_like(st_ref)

    nch = RET_TB // RET_CHUNK
    for cc in (reversed(range(nch)) if rev else range(nch)):
        rows = slice(cc * RET_CHUNK, (cc + 1) * RET_CHUNK)
        cos = cos_ref[rows, :]
        sin = sin_ref[rows, :]
        for h in range(RET_HEADS):
            lo = slice(h * RET_DK, h * RET_DK + half)
            hi = slice(h * RET_DK + half, (h + 1) * RET_DK)
            hv = slice(h * RET_DV, (h + 1) * RET_DV)
            q1, q2 = q_ref[rows, lo], q_ref[rows, hi]
            k1, k2 = k_ref[rows, lo], k_ref[rows, hi]
            qr = jnp.concatenate([q1 * cos - q2 * sin, q2 * cos + q1 * sin], axis=-1)
            kr = jnp.concatenate([k1 * cos - k2 * sin, k2 * cos + k1 * sin], axis=-1) * (RET_DK ** -0.5)
            qb = qr.astype(BF16)
            vb = v_ref[rows, hv].astype(BF16)
            att = lax.dot_general(qb, kr.astype(BF16), NT_DIMS, preferred_element_type=F32) * dm_ref[h]
            o = _dot(att.astype(BF16), vb) + _dot(qb, st_ref[h].astype(BF16)) * qd_ref[h]
            upd = lax.dot_general((kr * kd_ref[h]).astype(BF16), vb, TN_DIMS, preferred_element_type=F32)
            st_ref[h] = st_ref[h] * cd_ref[h] + upd
            if rev:
                y = _head_norm_gate(of_ref[rows, hv] + o, g_ref[rows, hv], ng_ref[:, hv])
                out_ref[rows, hv] = y.astype(out_ref.dtype)
            else:
                out_ref[rows, hv] = o


def _ret(h, cos, sin, dm, qd, kd, cd, nb, seqlen, rev, of=None, norm_g=None):
    n = nb * seqlen
    nblk = seqlen // RET_TB
    w = RET_HEADS * RET_DK

    def pos(b, c):
        return (nblk - 1 - c) if rev else c

    def blk(b, c):
        return b * nblk + pos(b, c)

    const3 = lambda b, c: (0, 0, 0)
    in_specs = [
        pl.BlockSpec((RET_TB, w), lambda b, c: (blk(b, c), 0)),
        pl.BlockSpec((RET_TB, w), lambda b, c: (blk(b, c), 1)),
        pl.BlockSpec((RET_TB, w), lambda b, c: (blk(b, c), 2)),
        pl.BlockSpec((RET_TB, RET_DK // 2), lambda b, c: (pos(b, c), 0)),
        pl.BlockSpec((RET_TB, RET_DK // 2), lambda b, c: (pos(b, c), 0)),
        pl.BlockSpec((RET_HEADS, RET_CHUNK, RET_CHUNK), const3),
        pl.BlockSpec((RET_HEADS, RET_CHUNK, 1), const3),
        pl.BlockSpec((RET_HEADS, RET_CHUNK, 1), const3),
        pl.BlockSpec((RET_HEADS, 1, RET_DV), const3),
    ]
    args = [h, h, h, cos, sin, dm, qd, kd, cd]
    if rev:
        in_specs += [
            pl.BlockSpec((RET_TB, w), lambda b, c: (blk(b, c), 0)),
            pl.BlockSpec((RET_TB, w), lambda b, c: (blk(b, c), 3)),
            pl.BlockSpec((1, w), lambda b, c: (0, 0)),
        ]
        args += [of, h, norm_g]
    return pl.pallas_call(
        functools.partial(_ret_kernel, rev),
        grid=(nb, nblk),
        in_specs=in_specs,
        out_specs=pl.BlockSpec((RET_TB, w), lambda b, c: (blk(b, c), 0)),
        out_shape=jax.ShapeDtypeStruct((n, w), BF16 if rev else F32),
        scratch_shapes=[pltpu.VMEM((RET_HEADS, RET_DK, RET_DV), F32)],
        compiler_params=_params("parallel", "arbitrary"),
        name="ret_bwd" if rev else "ret_fwd",
    )(*args)


def _ret_consts(log_g, rev):
    c = RET_CHUNK
    idx = jnp.arange(c, dtype=F32)
    lg = log_g[:, None]
    if rev:
        diff = idx[None, :] - idx[:, None]
        keep = diff >= 0
        qd = jnp.exp(lg * (c - idx))
        kd = jnp.exp(lg * idx)
    else:
        diff = idx[:, None] - idx[None, :]
        keep = diff > 0
        qd = jnp.exp(lg * (idx + 1.0))
        kd = jnp.exp(lg * (c - 1.0 - idx))
    dm = jnp.where(keep[None], jnp.exp(log_g[:, None, None] * jnp.maximum(diff, 0.0)[None]), 0.0)
    cd = jnp.broadcast_to(jnp.exp(log_g * c)[:, None, None], (RET_HEADS, 1, RET_DV))
    return dm, qd[:, :, None], kd[:, :, None], cd


SGU_TB = 512


def _gelu(x):
    return 0.5 * x * (1.0 + lax.erf(x * (2.0 ** -0.5)))


def _sgu_kernel(u_ref, z_ref, lg_ref, lb_ref, ws_ref, bs_ref, out_ref):
    gw = SGU_CH // SGU_GROUPS
    for ci in range(SGU_TB // SGU_CHUNK):
        rows = slice(ci * SGU_CHUNK, (ci + 1) * SGU_CHUNK)
        u = _gelu(u_ref[rows, :])
        zn = _ln_rows(_gelu(z_ref[rows, :]), lg_ref[...], lb_ref[...]).astype(BF16)
        for g in range(SGU_GROUPS):
            cols = slice(g * gw, (g + 1) * gw)
            sp = _dot(ws_ref[g], zn[:, cols]) + bs_ref[g]
            out_ref[rows, cols] = (u[:, cols] * sp).astype(out_ref.dtype)


def _sgu(h, lg, lb, ws, bs, n, col0):
    const3 = lambda i: (0, 0, 0)
    return pl.pallas_call(
        _sgu_kernel,
        grid=(n // SGU_TB,),
        in_specs=[
            pl.BlockSpec((SGU_TB, SGU_CH), lambda i: (i, col0)),
            pl.BlockSpec((SGU_TB, SGU_CH), lambda i: (i, col0 + 1)),
            pl.BlockSpec((1, SGU_CH), lambda i: (0, 0)),
            pl.BlockSpec((1, SGU_CH), lambda i: (0, 0)),
            pl.BlockSpec((SGU_GROUPS, SGU_CHUNK, SGU_CHUNK), const3),
            pl.BlockSpec((SGU_GROUPS, SGU_CHUNK, 1), const3),
        ],
        out_specs=pl.BlockSpec((SGU_TB, SGU_CH), lambda i: (i, 0)),
        out_shape=jax.ShapeDtypeStruct((n, SGU_CH), BF16),
        compiler_params=_params("parallel"),
        name="sgu",
    )(h, h, lg, lb, ws, bs)


OUT_BM = 512
OUT_RT = 32


def _outproj_kernel(a_ref, b_ref, w_ref, x_ref, g_ref, bb_ref, wr_ref, x1_ref, res_ref, p_ref):
    x1_ref[...] = _dot(jnp.concatenate([a_ref[...], b_ref[...]], axis=-1), w_ref[...])

    def norm(i, carry):
        rows = pl.ds(pl.multiple_of(i * OUT_RT, OUT_RT), OUT_RT)
        y = _ln_rows(ALPHA * x_ref[rows, :] + x1_ref[rows, :], g_ref[...], bb_ref[...])
        x1_ref[rows, :] = y
        res_ref[rows, :] = ALPHA * y
        return carry

    lax.fori_loop(0, OUT_BM // OUT_RT, norm, 0)
    logits = lax.dot_general(wr_ref[...], x1_ref[...], NT_DIMS, precision=lax.Precision.HIGHEST,
                             preferred_element_type=F32)
    e = jnp.exp(logits - jnp.max(logits, axis=0, keepdims=True))
    p_ref[...] = e / jnp.sum(e, axis=0, keepdims=True)


def _outproj(a, b, w, x, g, bb, wr_t):
    n = x.shape[0]
    half = D_MODEL // 2
    row = lambda i: (0, 0)
    return pl.pallas_call(
        _outproj_kernel,
        grid=(n // OUT_BM,),
        in_specs=[
            pl.BlockSpec((OUT_BM, half), lambda i: (i, 0)),
            pl.BlockSpec((OUT_BM, half), lambda i: (i, 0)),
            pl.BlockSpec((D_MODEL, D_MODEL), row, pipeline_mode=pl.Buffered(1)),
            pl.BlockSpec((OUT_BM, D_MODEL), lambda i: (i, 0)),
            pl.BlockSpec((1, D_MODEL), row),
            pl.BlockSpec((1, D_MODEL), row),
            pl.BlockSpec((N_EXPERTS, D_MODEL), row),
        ],
        out_specs=[pl.BlockSpec((OUT_BM, D_MODEL), lambda i: (i, 0)),
                   pl.BlockSpec((OUT_BM, D_MODEL), lambda i: (i, 0)),
                   pl.BlockSpec((N_EXPERTS, OUT_BM), lambda i: (0, i))],
        out_shape=[jax.ShapeDtypeStruct((n, D_MODEL), F32),
                   jax.ShapeDtypeStruct((n, D_MODEL), F32),
                   jax.ShapeDtypeStruct((N_EXPERTS, n), F32)],
        compiler_params=_params("parallel"),
        name="outproj_ln_router",
    )(a, b, w, x, g, bb, wr_t)


def _prefix_mats(nblk):
    r = lax.broadcasted_iota(I32, (LANES, LANES), 0)
    cidx = lax.broadcasted_iota(I32, (LANES, LANES), 1)
    upper = (r <= cidx).astype(BF16)
    ones = jnp.ones((LANES, LANES), BF16)
    br = lax.broadcasted_iota(I32, (nblk, nblk), 0)
    bc = lax.broadcasted_iota(I32, (nblk, nblk), 1)
    before = (bc < br).astype(BF16)
    return upper, ones, before


def _sum_rest(x):
    return jnp.sum(jnp.sum(x, axis=2, keepdims=True), axis=1, keepdims=True)


def _select_kernel(cap, p_ref, sel_ref):
    nblk = p_ref.shape[1]
    bits = pltpu.bitcast(p_ref[...], I32)

    def search(i, thr):
        cand = thr | lax.shift_left(jnp.int32(1), 30 - i)
        cnt = _sum_rest((bits >= cand).astype(I32))
        return jnp.where(cnt >= cap, cand, thr)

    thr = lax.fori_loop(0, 31, search, jnp.zeros((N_EXPERTS, 1, 1), I32))
    need = (cap - _sum_rest((bits > thr).astype(I32))).astype(F32)
    upper, ones, before = _prefix_mats(nblk)
    eq = (bits == thr).astype(F32)
    eq2 = eq.reshape(N_EXPERTS * nblk, LANES).astype(BF16)
    incl = _dot(eq2, upper)
    tot = _dot(eq2, ones).astype(BF16)
    for e in range(N_EXPERTS):
        rows = slice(e * nblk, (e + 1) * nblk)
        eq_before = _dot(before, tot[rows]) + incl[rows] - eq[e]
        sel_ref[e] = jnp.where(bits[e] > thr[e], 1.0, jnp.where(eq_before < need[e], eq[e], 0.0))


def _select(probs3, cap):
    full = pl.BlockSpec(probs3.shape, lambda i: (0, 0, 0))
    return pl.pallas_call(
        functools.partial(_select_kernel, cap),
        grid=(1,),
        in_specs=[full],
        out_specs=full,
        out_shape=jax.ShapeDtypeStruct(probs3.shape, F32),
        compiler_params=_params("arbitrary"),
        name="expert_select",
    )(probs3)


COMPACT_RT = 512


def _compact_kernel(cap, sel_ref, p_ref, out_ref, col_s):
    nblk = sel_ref.shape[0]
    upper, ones, before = _prefix_mats(nblk)
    selb = sel_ref[...].astype(BF16)
    rank = _dot(selb, upper)
    tot = _dot(selb, ones)
    through = _dot(before, tot.astype(BF16)) + tot
    through_row = jnp.transpose(through)[0:1, :]
    lane = lax.broadcasted_iota(I32, (1, LANES), 1)
    w = jnp.where(lax.broadcasted_iota(I32, (nblk, LANES), 1) == 0, 1.0, tot).astype(BF16)
    blk_lane = lax.broadcasted_iota(I32, (1, nblk), 1).astype(F32)
    rank_b = rank.astype(BF16)

    def chunk(i, carry):
        r0 = pl.multiple_of(i * COMPACT_RT, COMPACT_RT)
        r = (r0 + lax.broadcasted_iota(I32, (COMPACT_RT, 1), 0)).astype(F32)
        passed = (through_row <= r).astype(BF16)
        res = _dot(passed, w)
        blk = res[:, 0:1]
        onehot = blk_lane == blk
        target = r - res[:, 1:2]
        ranks = _dot(onehot.astype(BF16), rank_b)
        off = _dot((ranks <= target).astype(BF16), ones)
        probs = _dot_f32(onehot.astype(F32), p_ref[...])
        gate = jnp.sum(jnp.where(lane.astype(F32) == off, probs, 0.0), axis=-1, keepdims=True)
        col_s[pl.ds(r0, COMPACT_RT), :] = jnp.where(lane == 0, blk * LANES + off, gate)
        return carry

    lax.fori_loop(0, cap // COMPACT_RT, chunk, 0)
    out_ref[...] = jnp.transpose(col_s[...])[0:SUBLANES, :]


def _compact(sel3, probs3, cap):
    nblk = sel3.shape[1]
    blk = pl.BlockSpec((None, nblk, LANES), lambda e: (e, 0, 0))
    return pl.pallas_call(
        functools.partial(_compact_kernel, cap),
        grid=(N_EXPERTS,),
        in_specs=[blk, blk],
        out_specs=pl.BlockSpec((None, SUBLANES, cap), lambda e: (e, 0, 0)),
        out_shape=jax.ShapeDtypeStruct((N_EXPERTS, SUBLANES, cap), F32),
        scratch_shapes=[pltpu.VMEM((cap, LANES), F32)],
        compiler_params=_params("parallel"),
        name="expert_compact",
    )(sel3, probs3)


ROWS = 256
ISSUE_UNROLL = 32
UP_BM = 512
UP_FT = D_EXPERT // 2
DOWN_BM = 512


def _row_copy(src_hbm, dst, idx_ref, sem, gather):
    def issue(r, carry):
        t = idx_ref[0, r]
        if gather:
            pltpu.make_async_copy(src_hbm.at[pl.ds(t, 1)], dst.at[pl.ds(r, 1)], sem).start()
        else:
            pltpu.make_async_copy(dst.at[pl.ds(r, 1)], src_hbm.at[pl.ds(t, 1)], sem).start()
        return carry

    lax.fori_loop(0, ROWS, issue, 0, unroll=ISSUE_UNROLL)


def _wait_rows(hbm, buf, sem, gather):
    if gather:
        pltpu.make_async_copy(hbm.at[pl.ds(0, ROWS)], buf, sem).wait()
    else:
        pltpu.make_async_copy(buf, hbm.at[pl.ds(0, ROWS)], sem).wait()


def _gather_kernel(idx_ref, nxt_ref, x_hbm, out_ref, buf, sem):
    i = pl.program_id(0)
    slot = i % 2

    @pl.when(i == 0)
    def _():
        _row_copy(x_hbm, buf.at[0], idx_ref, sem.at[0], True)

    @pl.when(i + 1 < pl.num_programs(0))
    def _():
        _row_copy(x_hbm, buf.at[1 - slot], nxt_ref, sem.at[1 - slot], True)

    _wait_rows(x_hbm, buf.at[slot], sem.at[slot], True)
    out_ref[...] = buf[slot].astype(out_ref.dtype)


def _gather_rows(idx3, x):
    nsteps = idx3.shape[0]
    return pl.pallas_call(
        _gather_kernel,
        grid=(nsteps,),
        in_specs=[pl.BlockSpec((None, 1, ROWS), lambda i: (i, 0, 0), memory_space=pltpu.SMEM),
                  pl.BlockSpec((None, 1, ROWS), lambda i: (jnp.minimum(i + 1, nsteps - 1), 0, 0),
                               memory_space=pltpu.SMEM),
                  pl.BlockSpec(memory_space=pl.ANY)],
        out_specs=pl.BlockSpec((ROWS, D_MODEL), lambda i: (i, 0)),
        out_shape=jax.ShapeDtypeStruct((nsteps * ROWS, D_MODEL), BF16),
        scratch_shapes=[pltpu.VMEM((2, ROWS, D_MODEL), F32), pltpu.SemaphoreType.DMA((2,))],
        compiler_params=_params("arbitrary"),
        name="moe_gather",
    )(idx3, idx3, x)


def _up_kernel(xe_ref, wg_ref, wu_ref, h_ref):
    xe = xe_ref[...]
    a = _dot(xe, wg_ref[...])
    h_ref[...] = (a * _sigmoid(a) * _dot(xe, wu_ref[...])).astype(h_ref.dtype)


def _expert_up(xe, wg, wu, cap, layer):
    nbm = cap // UP_BM
    return pl.pallas_call(
        _up_kernel,
        grid=(N_EXPERTS, D_EXPERT // UP_FT, nbm),
        in_specs=[pl.BlockSpec((UP_BM, D_MODEL), lambda e, f, i: (e * nbm + i, 0)),
                  pl.BlockSpec((None, None, D_MODEL, UP_FT), lambda e, f, i: (layer, e, 0, f)),
                  pl.BlockSpec((None, None, D_MODEL, UP_FT), lambda e, f, i: (layer, e, 0, f))],
        out_specs=pl.BlockSpec((UP_BM, UP_FT), lambda e, f, i: (e * nbm + i, f)),
        out_shape=jax.ShapeDtypeStruct((N_EXPERTS * cap, D_EXPERT), BF16),
        compiler_params=_params("parallel", "parallel", "arbitrary"),
        name="expert_up",
    )(xe, wg, wu)


def _down_kernel(h_ref, wd_ref, gate_ref, y_ref):
    y_ref[...] = _dot(h_ref[...], wd_ref[...]) * gate_ref[...]


def _expert_down(hid, wd, gate_col, cap, layer):
    nbm = cap // DOWN_BM
    return pl.pallas_call(
        _down_kernel,
        grid=(N_EXPERTS, nbm),
        in_specs=[pl.BlockSpec((DOWN_BM, D_EXPERT), lambda e, i: (e * nbm + i, 0)),
                  pl.BlockSpec((None, None, D_EXPERT, D_MODEL), lambda e, i: (layer, e, 0, 0)),
                  pl.BlockSpec((DOWN_BM, 1), lambda e, i: (e * nbm + i, 0))],
        out_specs=pl.BlockSpec((DOWN_BM, D_MODEL), lambda e, i: (e * nbm + i, 0)),
        out_shape=jax.ShapeDtypeStruct((N_EXPERTS * cap, D_MODEL), F32),
        compiler_params=_params("parallel", "arbitrary"),
        name="expert_down",
    )(hid, wd, gate_col)


def _scatter_kernel(steps_per_expert, idx_ref, nxt_ref, y_ref, acc_in, acc_hbm, buf, sem_in, sem_out):
    del acc_in
    i = pl.program_id(0)
    slot = i % 2
    pos = i % steps_per_expert

    @pl.when(pos == 0)
    def _():
        _row_copy(acc_hbm, buf.at[slot], idx_ref, sem_in.at[slot], True)

    _wait_rows(acc_hbm, buf.at[slot], sem_in.at[slot], True)

    @pl.when(pos + 1 < steps_per_expert)
    def _():
        _row_copy(acc_hbm, buf.at[1 - slot], nxt_ref, sem_in.at[1 - slot], True)

    buf[slot] = buf[slot] + y_ref[...]
    _row_copy(acc_hbm, buf.at[slot], idx_ref, sem_out, False)
    _wait_rows(acc_hbm, buf.at[slot], sem_out, False)


def _scatter_add(idx3, y, acc, steps_per_expert):
    nsteps = idx3.shape[0]
    return pl.pallas_call(
        functools.partial(_scatter_kernel, steps_per_expert),
        grid=(nsteps,),
        in_specs=[pl.BlockSpec((None, 1, ROWS), lambda i: (i, 0, 0), memory_space=pltpu.SMEM),
                  pl.BlockSpec((None, 1, ROWS), lambda i: (jnp.minimum(i + 1, nsteps - 1), 0, 0),
                               memory_space=pltpu.SMEM),
                  pl.BlockSpec((ROWS, D_MODEL), lambda i: (i, 0)),
                  pl.BlockSpec(memory_space=pl.ANY)],
        out_specs=pl.BlockSpec(memory_space=pl.ANY),
        out_shape=jax.ShapeDtypeStruct(acc.shape, F32),
        scratch_shapes=[pltpu.VMEM((2, ROWS, D_MODEL), F32), pltpu.SemaphoreType.DMA((2,)),
                        pltpu.SemaphoreType.DMA],
        input_output_aliases={3: 0},
        compiler_params=_params("arbitrary"),
        name="moe_scatter_add",
    )(idx3, idx3, y, acc)


FIN_BM = 512


def _final_kernel(y_ref, g_ref, b_ref, o_ref):
    o_ref[...] = _ln_rows(y_ref[...], g_ref[...], b_ref[...])


def _final_ln(y, g, b):
    n = y.shape[0]
    tile = pl.BlockSpec((FIN_BM, D_MODEL), lambda i: (i, 0))
    row = pl.BlockSpec((1, D_MODEL), lambda i: (0, 0))
    return pl.pallas_call(
        _final_kernel,
        grid=(n // FIN_BM,),
        in_specs=[tile, row, row],
        out_specs=tile,
        out_shape=jax.ShapeDtypeStruct((n, D_MODEL), F32),
        compiler_params=_params("parallel"),
        name="ffn_residual_ln",
    )(y, g, b)


def _route(probs_t):
    n = probs_t.shape[1]
    cap = CAP_FACTOR * n // N_EXPERTS
    probs3 = probs_t.reshape(N_EXPERTS, n // LANES, LANES)
    packed = _compact(_select(probs3, cap), probs3, cap)
    return packed[:, 0, :].astype(I32), packed[:, 1, :], cap


def _moe(resid, x1, probs_t, wg, wu, wd, layer):
    idx, gate, cap = _route(probs_t)
    idx3 = idx.reshape(N_EXPERTS * cap // ROWS, 1, ROWS)
    xe = _gather_rows(idx3, x1)
    hid = _expert_up(xe, wg, wu, cap, layer)
    y = _expert_down(hid, wd, gate.reshape(N_EXPERTS * cap, 1), cap, layer)
    return _scatter_add(idx3, y, resid, cap // ROWS)


def _rotary_tables(seqlen):
    half = RET_DK // 2
    inv = 1.0 / (10000.0 ** jnp.linspace(0.0, 1.0, half, dtype=F32))
    ang = jnp.arange(seqlen, dtype=F32)[:, None] * inv[None, :]
    return jnp.cos(ang), jnp.sin(ang)


def _even_layer(x, p, i, nb, seqlen):
    h = _matmul(x, p["ev_w_main"][i], 1024, 1024, "in_proj_even")
    lr = _matmul(x, p["ev_w_lr"][i], 1024, LANES, "in_proj_lowrank")
    of = _gla(h, lr, p["ev_wz"][i, 0], p["ev_bz"][i, 0], _gla_tri(False), nb, seqlen, False)
    out_a = _gla(h, lr, p["ev_wz"][i, 1], p["ev_bz"][i, 1], _gla_tri(True), nb, seqlen, True,
                 of=of, norm_g=p["ev_gla_norm_g"][i])
    out_b = _conv(h, p["ev_conv_w"][i], p["ev_conv_b"][i], p["ev_conv_norm_g"][i], p["ev_conv_norm_b"][i],
                  nb, seqlen, 3)
    return out_a, out_b, p["ev_w_out"][i]


def _odd_layer(x, p, i, nb, seqlen):
    n = x.shape[0]
    h = _matmul(x, p["od_w_in"][i], 1024, 1024, "in_proj_odd")
    cos, sin = _rotary_tables(seqlen)
    hh = jnp.arange(RET_HEADS, dtype=F32)
    cf = _ret_consts(jnp.log1p(-jnp.exp2(-5.0 - 2.0 * hh)), False)
    cb = _ret_consts(jnp.log1p(-jnp.exp2(-6.0 - 2.0 * hh)), True)
    of = _ret(h, cos, sin, *cf, nb, seqlen, False)
    out_c = _ret(h, cos, sin, *cb, nb, seqlen, True, of=of, norm_g=p["od_ret_norm_g"][i])
    out_d = _sgu(h, p["od_sgu_norm_g"][i], p["od_sgu_norm_b"][i], p["od_w_spatial"][i], p["od_b_spatial"][i], n, 4)
    return out_c, out_d, p["od_w_out"][i]


def _trunk(x3, p):
    nb, seqlen, _ = x3.shape
    x = x3.reshape(nb * seqlen, D_MODEL)
    for layer in range(DEPTH):
        i = layer // 2
        mix = _even_layer if layer % 2 == 0 else _odd_layer
        a, b, w_out = mix(x, p, i, nb, seqlen)
        x1, resid, probs_t = _outproj(a, b, w_out, x, p["ln_mix_g"][layer], p["ln_mix_b"][layer],
                                      p["w_router_t"][layer])
        y = _moe(resid, x1, probs_t, p["w_gate"], p["w_up"], p["w_down"], layer)
        x = _final_ln(y, p["ln_ffn_g"][layer], p["ln_ffn_b"][layer])
    return x.reshape(nb, seqlen, D_MODEL)


def kernel(x_prompt, x_sample, ev_w_in, ev_w_gate_up, ev_b_gate, ev_gla_norm_g, ev_conv_w, ev_conv_b, ev_conv_norm_g, ev_conv_norm_b, ev_w_out, od_w_in, od_ret_norm_g, od_sgu_norm_g, od_sgu_norm_b, od_w_spatial, od_b_spatial, od_w_out, ln_mix_g, ln_mix_b, ln_ffn_g, ln_ffn_b, w_router, w_gate, w_up, w_down):
    n_even = ev_w_in.shape[0]
    lr0 = 2 * GLA_QK + 2 * GLA_W
    cv0 = lr0 + 2 * GLA_RANK
    row = lambda a: a[:, None, :]
    wz = jnp.zeros((n_even, 2, LANES, GLA_QK), F32)
    for z in range(2):
        wz = wz.at[:, z, z * GLA_RANK:(z + 1) * GLA_RANK, :].set(ev_w_gate_up[:, z])
    p = {
        "ev_w_main": jnp.concatenate([ev_w_in[:, :, :lr0], ev_w_in[:, :, cv0:]], axis=-1).astype(BF16),
        "ev_w_lr": jnp.pad(ev_w_in[:, :, lr0:cv0], ((0, 0), (0, 0), (0, LANES - 2 * GLA_RANK))).astype(BF16),
        "ev_wz": wz.astype(BF16),
        "ev_bz": ev_b_gate[:, :, None, :],
        "ev_gla_norm_g": row(ev_gla_norm_g),
        "ev_conv_w": jnp.pad(ev_conv_w, ((0, 0), (0, 1), (0, 0))),
        "ev_conv_b": row(ev_conv_b),
        "ev_conv_norm_g": row(ev_conv_norm_g),
        "ev_conv_norm_b": row(ev_conv_norm_b),
        "ev_w_out": ev_w_out.astype(BF16),
        "od_w_in": od_w_in.astype(BF16),
        "od_ret_norm_g": row(od_ret_norm_g),
        "od_sgu_norm_g": row(od_sgu_norm_g),
        "od_sgu_norm_b": row(od_sgu_norm_b),
        "od_w_spatial": od_w_spatial.astype(BF16),
        "od_b_spatial": od_b_spatial[:, :, :, None],
        "od_w_out": od_w_out.astype(BF16),
        "ln_mix_g": row(ln_mix_g), "ln_mix_b": row(ln_mix_b),
        "ln_ffn_g": row(ln_ffn_g), "ln_ffn_b": row(ln_ffn_b),
        "w_router_t": jnp.swapaxes(w_router, 1, 2),
        "w_gate": w_gate.astype(BF16), "w_up": w_up.astype(BF16), "w_down": w_down.astype(BF16),
    }
    return _trunk(x_prompt, p), _trunk(x_sample, p)
```
